```python
import math
import jax, jax.numpy as jnp
from jax import lax
import numpy as np

D_MODEL = 2048
BATCH = 32
SEQ = 256
DEPTH = 2
DEC_BATCH = 2
DEC_SEQ = 2048
PAST_LEN = 256

GRID_W = 64
HEAD_DIM = 128
N_HEADS_A = 8
N_KV_A = 2
G_A = N_HEADS_A // N_KV_A
N_HEADS_B = 8
N_KV_B = 2
G_B = N_HEADS_B // N_KV_B
WINDOW = 128
BLOCK = 128
D_FF = 5632
CONV_K = 3
ROPE_THETA = 10000.0
LN_EPS = 1e-6
ALPHA = (2.0 * DEPTH) ** 0.25
BETA = (8.0 * DEPTH) ** -0.25
SCALE = HEAD_DIM ** -0.5
NEG = -1e30
SIZES = (N_HEADS_A * HEAD_DIM, N_KV_A * HEAD_DIM, N_KV_A * HEAD_DIM,
         N_HEADS_B * HEAD_DIM, N_KV_B * HEAD_DIM, N_KV_B * HEAD_DIM)
D_IN = sum(SIZES)
SPLITS = tuple(int(s) for s in np.cumsum(SIZES)[:-1])

kernel_name = "hybrid_dit_window_sink_axialrope_convffn_step"


def layer_norm(x, g=None, b=None):
    xf = x.astype(jnp.float32)
    mu = jnp.mean(xf, axis=-1, keepdims=True)
    var = jnp.mean(jnp.square(xf - mu), axis=-1, keepdims=True)
    y = (xf - mu) * lax.rsqrt(var + LN_EPS)
    if g is not None:
        y = y * g.astype(jnp.float32) + b.astype(jnp.float32)
    return y.astype(x.dtype)


def rms_norm(x, g):
    xf = x.astype(jnp.float32)
    y = xf * lax.rsqrt(jnp.mean(jnp.square(xf), axis=-1, keepdims=True) + LN_EPS)
    return (y * g.astype(jnp.float32)).astype(x.dtype)


def adaln(cvec, w, b):
    mod = (jax.nn.silu(cvec) @ w + b).reshape(cvec.shape[0], 1, 6, D_MODEL)
    return [mod[:, :, i] for i in range(6)]


def modulate(x, shift, scale):
    return layer_norm(x) * (1.0 + scale) + shift


def post_norm(x, out, gate, g, b):
    return layer_norm(ALPHA * x + gate * out, g, b)


def rope_tables(n):
    rows = n // GRID_W
    row = jnp.repeat(jnp.arange(rows), GRID_W).astype(jnp.float32)
    col = jnp.tile(jnp.arange(GRID_W), rows).astype(jnp.float32)
    q4 = HEAD_DIM // 4
    freq = ROPE_THETA ** (-jnp.arange(q4, dtype=jnp.float32) / q4)
    ang = jnp.stack([row[:, None] * freq, col[:, None] * freq], axis=1)
    return jnp.cos(ang), jnp.sin(ang)


def apply_rope(x, cos, sin):
    xs = x.astype(jnp.float32).reshape(x.shape[:-1] + (2, 2, HEAD_DIM // 4))
    x1, x2 = xs[..., 0, :], xs[..., 1, :]
    c, s = cos[None, :, None], sin[None, :, None]
    out = jnp.stack([x1 * c - x2 * s, x2 * c + x1 * s], axis=-2)
    return out.reshape(x.shape).astype(x.dtype)


def project(h, w_in, qn_g, kn_g):
    B, N = h.shape[0], h.shape[1]
    qa, ka, va, qb, kb, vb = jnp.split(h @ w_in, SPLITS, axis=-1)
    qa = qa.reshape(B, N, N_HEADS_A, HEAD_DIM)
    ka = ka.reshape(B, N, N_KV_A, HEAD_DIM)
    va = va.reshape(B, N, N_KV_A, HEAD_DIM)
    qb = rms_norm(qb.reshape(B, N, N_HEADS_B, HEAD_DIM), qn_g)
    kb = rms_norm(kb.reshape(B, N, N_KV_B, HEAD_DIM), kn_g)
    vb = vb.reshape(B, N, N_KV_B, HEAD_DIM)
    return qa, ka, va, qb, kb, vb


def dense_attend(q, k, v, sink=None):
    s = jnp.einsum('bqkgd,bskd->bkgqs', q, k).astype(jnp.float32) * SCALE
    if sink is not None:
        col = jnp.broadcast_to(sink.astype(jnp.float32)[None, :, :, None, None], s.shape[:-1] + (1,))
        s = jnp.concatenate([s, col], axis=-1)
    p = jax.nn.softmax(s, axis=-1)
    if sink is not None:
        p = p[..., :-1]
    return jnp.einsum('bkgqs,bskd->bqkgd', p.astype(v.dtype), v)


def _bands(t):
    B, N = t.shape[0], t.shape[1]
    nb = N // BLOCK
    tp = jnp.pad(t, ((0, 0), (BLOCK, BLOCK), (0, 0), (0, 0)))
    tb = tp.reshape(B, nb + 2, BLOCK, t.shape[2], HEAD_DIM)
    return jnp.concatenate([tb[:, :-2], tb[:, 1:-1], tb[:, 2:]], axis=2)


def window_attend_latent(q, k, v, ck, cv, sink):
    B, N = q.shape[0], q.shape[1]
    nb = N // BLOCK
    L = 3 * BLOCK
    P = ck.shape[1]
    qb = q.reshape(B, nb, BLOCK, N_KV_A, G_A, HEAD_DIM)
    kb, vb = _bands(k), _bands(v)
    s_loc = jnp.einsum('bnqkgd,bnskd->bnkgqs', qb, kb).astype(jnp.float32) * SCALE
    s_ctx = jnp.einsum('bnqkgd,bskd->bnkgqs', qb, ck).astype(jnp.float32) * SCALE
    a = jnp.arange(BLOCK)[:, None]
    j = jnp.arange(L)[None, :]
    near = jnp.abs(a - j + BLOCK) <= WINDOW
    kpos = jnp.arange(nb)[:, None, None] * BLOCK - BLOCK + j[None]
    mask = near[None] & (kpos >= 0) & (kpos < N)
    s_loc = jnp.where(mask[None, :, None, None], s_loc, NEG)
    s_sink = jnp.broadcast_to(sink.astype(jnp.float32)[None, None, :, :, None, None], s_loc.shape[:-1] + (1,))
    p = jax.nn.softmax(jnp.concatenate([s_loc, s_ctx, s_sink], axis=-1), axis=-1).astype(v.dtype)
    o = (jnp.einsum('bnkgqs,bnskd->bnqkgd', p[..., :L], vb)
         + jnp.einsum('bnkgqs,bskd->bnqkgd', p[..., L:L + P], cv))
    return o.reshape(B, N, N_HEADS_A * HEAD_DIM)


def global_attend_latent(q, k, v, ck, cv):
    B, N = q.shape[0], q.shape[1]
    nb = N // BLOCK
    keys = jnp.concatenate([ck, k], axis=1)
    vals = jnp.concatenate([cv, v], axis=1)
    qb = jnp.moveaxis(q.reshape(B, nb, BLOCK, N_KV_B, G_B, HEAD_DIM), 1, 0)
    out = lax.map(lambda qq: dense_attend(qq, keys, vals), qb)
    return jnp.moveaxis(out, 0, 1).reshape(B, N, N_HEADS_B * HEAD_DIM)


def mixers_context(h, w_in, qn_g, kn_g, sink):
    B, N = h.shape[0], h.shape[1]
    qa, ka, va, qb, kb, vb = project(h, w_in, qn_g, kn_g)
    oa = dense_attend(qa.reshape(B, N, N_KV_A, G_A, HEAD_DIM), ka, va, sink.reshape(N_KV_A, G_A))
    ob = dense_attend(qb.reshape(B, N, N_KV_B, G_B, HEAD_DIM), kb, vb)
    o = jnp.concatenate([oa.reshape(B, N, -1), ob.reshape(B, N, -1)], axis=-1)
    return o, jnp.stack([ka, va], axis=1), jnp.stack([kb, vb], axis=1)


def mixers_latent(h, cache_a, cache_b, w_in, qn_g, kn_g, sink):
    B, N = h.shape[0], h.shape[1]
    cos, sin = rope_tables(N)
    qa, ka, va, qb, kb, vb = project(h, w_in, qn_g, kn_g)
    qa, ka = apply_rope(qa, cos, sin), apply_rope(ka, cos, sin)
    qb, kb = apply_rope(qb, cos, sin), apply_rope(kb, cos, sin)
    oa = window_attend_latent(qa.reshape(B, N, N_KV_A, G_A, HEAD_DIM), ka, va,
                              cache_a[:, 0], cache_a[:, 1], sink.reshape(N_KV_A, G_A))
    ob = global_attend_latent(qb.reshape(B, N, N_KV_B, G_B, HEAD_DIM), kb, vb,
                              cache_b[:, 0], cache_b[:, 1])
    return jnp.concatenate([oa, ob], axis=-1)


def conv_ffn(h, w_up, conv_w, conv_b, w_down):
    u = h @ w_up
    up = jnp.pad(u, ((0, 0), (1, 1), (0, 0)))
    u = up[:, :-2] * conv_w[0] + up[:, 1:-1] * conv_w[1] + up[:, 2:] * conv_w[2] + conv_b
    val, gate = jnp.split(u, 2, axis=-1)
    return (jax.nn.silu(gate) * val) @ w_down


def setup_inputs(seed: int = 0) -> dict:
    key = jax.random.key(seed)
    ks = jax.random.split(key, 24)
    f32 = jnp.float32
    nrm = lambda k, shape, s: jax.random.normal(k, shape, f32) * s
    D, F = D_MODEL, D_FF
    return {
        'x_prompt': nrm(ks[0], (BATCH, SEQ, D), 1.0),
        'x_sample': nrm(ks[1], (DEC_BATCH, DEC_SEQ, D), 1.0),
        'cache_attn_a': nrm(ks[2], (DEC_BATCH, DEPTH, 2, PAST_LEN, N_KV_A, HEAD_DIM), 1.0),
        'cache_attn_b': nrm(ks[3], (DEC_BATCH, DEPTH, 2, PAST_LEN, N_KV_B, HEAD_DIM), 1.0),
        'c': nrm(ks[4], (DEC_BATCH, D), 1.0),
        'c_ctx': nrm(ks[5], (D,), 1.0),
        'w_ada': nrm(ks[6], (DEPTH, D, 6 * D), 0.5 * D ** -0.5),
        'b_ada': nrm(ks[7], (DEPTH, 6 * D), 0.02),
        'w_in': nrm(ks[8], (DEPTH, D, D_IN), D ** -0.5),
        'q_norm_g': 1.0 + nrm(ks[9], (DEPTH, HEAD_DIM), 0.02),
        'k_norm_g': 1.0 + nrm(ks[10], (DEPTH, HEAD_DIM), 0.02),
        'sink_a': nrm(ks[11], (DEPTH, N_HEADS_A), 0.5),
        'w_o': nrm(ks[12], (DEPTH, D, D), BETA * D ** -0.5),
        'ln1_g': 1.0 + nrm(ks[13], (DEPTH, D), 0.02),
        'ln1_b': nrm(ks[14], (DEPTH, D), 0.02),
        'w_up': nrm(ks[15], (DEPTH, D, 2 * F), D ** -0.5),
        'conv_w': nrm(ks[16], (DEPTH, CONV_K, 2 * F), CONV_K ** -0.5),
        'conv_b': nrm(ks[17], (DEPTH, 2 * F), 0.02),
        'w_down': nrm(ks[18], (DEPTH, F, D), BETA * F ** -0.5),
        'ln2_g': 1.0 + nrm(ks[19], (DEPTH, D), 0.02),
        'ln2_b': nrm(ks[20], (DEPTH, D), 0.02),
    }


def reference(x_prompt, x_sample, cache_attn_a, cache_attn_b, c, c_ctx, w_ada, b_ada, w_in,
              q_norm_g, k_norm_g, sink_a, w_o, ln1_g, ln1_b, w_up, conv_w, conv_b, w_down,
              ln2_g, ln2_b):
    y = x_prompt
    z = x_sample
    new_a, new_b = [], []
    for l in range(DEPTH):
        s1, sc1, g1, s2, sc2, g2 = adaln(c_ctx[None], w_ada[l], b_ada[l])
        h = modulate(y, s1, sc1)
        o, kv_a, kv_b = mixers_context(h, w_in[l], q_norm_g[l], k_norm_g[l], sink_a[l])
        y = post_norm(y, o @ w_o[l], g1, ln1_g[l], ln1_b[l])
        h = modulate(y, s2, sc2)
        y = post_norm(y, conv_ffn(h, w_up[l], conv_w[l], conv_b[l], w_down[l]), g2, ln2_g[l], ln2_b[l])
        new_a.append(kv_a)
        new_b.append(kv_b)
        s1, sc1, g1, s2, sc2, g2 = adaln(c, w_ada[l], b_ada[l])
        h = modulate(z, s1, sc1)
        o = mixers_latent(h, cache_attn_a[:, l], cache_attn_b[:, l], w_in[l], q_norm_g[l], k_norm_g[l], sink_a[l])
        z = post_norm(z, o @ w_o[l], g1, ln1_g[l], ln1_b[l])
        h = modulate(z, s2, sc2)
        z = post_norm(z, conv_ffn(h, w_up[l], conv_w[l], conv_b[l], w_down[l]), g2, ln2_g[l], ln2_b[l])
    new_cache_a = jnp.stack(new_a, axis=1)
    new_cache_b = jnp.stack(new_b, axis=1)
    return (y, z, new_cache_a, new_cache_b)
```

```python
import functools

import jax
import jax.numpy as jnp
import numpy as np
from jax import lax
from jax.experimental import pallas as pl
from jax.experimental.pallas import tpu as pltpu

D_MODEL = 2048
BATCH = 32
SEQ = 256
DEPTH = 2
DEC_BATCH = 2
DEC_SEQ = 2048
PAST_LEN = 256
GRID_W = 64
HEAD_DIM = 128
N_HEADS_A = 8
N_KV_A = 2
N_HEADS_B = 8
N_KV_B = 2
GROUP = 4
BLOCK = 128
D_FF = 5632
ROPE_THETA = 10000.0
LN_EPS = 1e-6
ALPHA = (2.0 * DEPTH) ** 0.25
SCALE = HEAD_DIM ** -0.5
NEG = -1e30

D_Q = (N_HEADS_A + N_HEADS_B) * HEAD_DIM
D_KV = 2 * (N_KV_A + N_KV_B) * HEAD_DIM
D_IN = D_Q + D_KV
OFF_QA, OFF_KA, OFF_VA = 0, 1024, 1280
OFF_QB, OFF_KB, OFF_VB = 1536, 2560, 2816

M_CTX = BATCH * SEQ
M_LAT = DEC_BATCH * DEC_SEQ
M_ALL = M_CTX + M_LAT
N_MOD_ROWS = 8

TM = 512
N_TILES = M_ALL // TM
N_CTX_TILES = M_CTX // TM
TILES_PER_LAT_SEQ = DEC_SEQ // TM
HALO = 16
TF = 512
N_F = D_FF // TF
TN_ADA = 1024
VMEM_LIMIT = 56 * 1024 * 1024

f32 = jnp.float32
bf16 = jnp.bfloat16


def _ln(x):
    mu = jnp.mean(x, axis=-1, keepdims=True)
    xc = x - mu
    var = jnp.mean(xc * xc, axis=-1, keepdims=True)
    return xc * lax.rsqrt(var + LN_EPS)


def _mod_index(layer, which):
    def index_map(i, *_):
        row = jnp.where(i < N_CTX_TILES, 0, 1 + (i - N_CTX_TILES) // TILES_PER_LAT_SEQ)
        return ((layer * N_MOD_ROWS + row) * 6 + which, 0, 0)
    return index_map


def _mod_spec(layer, which):
    return pl.BlockSpec((None, 1, D_MODEL), _mod_index(layer, which))


def _adaln_kernel(cv_ref, w_ref, b_ref, o_ref):
    cv = cv_ref[...]
    a = (cv * jax.nn.sigmoid(cv)).astype(bf16)
    o_ref[...] = jnp.dot(a, w_ref[...].astype(bf16), preferred_element_type=f32) + b_ref[...]


def _adaln(cvecs, w_ada, b_ada):
    return pl.pallas_call(
        _adaln_kernel,
        grid=(DEPTH, 6 * D_MODEL // TN_ADA),
        in_specs=[
            pl.BlockSpec((N_MOD_ROWS, D_MODEL), lambda l, n: (0, 0)),
            pl.BlockSpec((None, D_MODEL, TN_ADA), lambda l, n: (l, 0, n)),
            pl.BlockSpec((None, 1, TN_ADA), lambda l, n: (l, 0, n)),
        ],
        out_specs=pl.BlockSpec((None, N_MOD_ROWS, TN_ADA), lambda l, n: (l, 0, n)),
        out_shape=jax.ShapeDtypeStruct((DEPTH, N_MOD_ROWS, 6 * D_MODEL), f32),
        compiler_params=pltpu.CompilerParams(
            dimension_semantics=("arbitrary", "arbitrary"), vmem_limit_bytes=VMEM_LIMIT),
    )(cvecs, w_ada, b_ada.reshape(DEPTH, 1, 6 * D_MODEL))


def _qkv_kernel(x_ref, shift_ref, scale_ref, w_ref, qg_ref, kg_ref, rc_ref, ra_ref, rb_ref,
                q_ref, kv_ref):
    i = pl.program_id(0)
    h = _ln(x_ref[...]) * (1.0 + scale_ref[...]) + shift_ref[...]
    qkv = jnp.dot(h.astype(bf16), w_ref[...], preferred_element_type=f32)

    def head(col, gain, rope):
        xh = qkv[:, col:col + HEAD_DIM]
        if gain is not None:
            ms = jnp.mean(xh * xh, axis=-1, keepdims=True)
            xh = xh * lax.rsqrt(ms + LN_EPS) * gain
        if rope:
            xh = (xh * rc_ref[...] + pltpu.roll(xh, HEAD_DIM - 32, 1) * ra_ref[...]
                  + pltpu.roll(xh, 32, 1) * rb_ref[...])
        return xh

    def emit(rope):
        qg = qg_ref[...]
        kg = kg_ref[...]
        for hh in range(N_HEADS_A):
            q_ref[:, hh * HEAD_DIM:(hh + 1) * HEAD_DIM] = head(
                OFF_QA + hh * HEAD_DIM, None, rope).astype(bf16)
        for hh in range(N_HEADS_B):
            c0 = (N_HEADS_A + hh) * HEAD_DIM
            q_ref[:, c0:c0 + HEAD_DIM] = head(OFF_QB + hh * HEAD_DIM, qg, rope).astype(bf16)
        for j in range(N_KV_A):
            kv_ref[:, j * HEAD_DIM:(j + 1) * HEAD_DIM] = head(OFF_KA + j * HEAD_DIM, None, rope)
        kv_ref[:, 256:512] = qkv[:, OFF_VA:OFF_VA + 256]
        for j in range(N_KV_B):
            kv_ref[:, 512 + j * HEAD_DIM:512 + (j + 1) * HEAD_DIM] = head(
                OFF_KB + j * HEAD_DIM, kg, rope)
        kv_ref[:, 768:1024] = qkv[:, OFF_VB:OFF_VB + 256]

    @pl.when(i < N_CTX_TILES)
    def _():
        emit(False)

    @pl.when(i >= N_CTX_TILES)
    def _():
        emit(True)


def _qkv(x, mod, w_in, qg, kg, rope_c, rope_a, rope_b, layer):
    rope_spec = pl.BlockSpec(
        (TM, HEAD_DIM), lambda i: (jnp.maximum(i - N_CTX_TILES, 0) % TILES_PER_LAT_SEQ, 0))
    return pl.pallas_call(
        _qkv_kernel,
        grid=(N_TILES,),
        in_specs=[
            pl.BlockSpec((TM, D_MODEL), lambda i: (i, 0)),
            _mod_spec(layer, 0),
            _mod_spec(layer, 1),
            pl.BlockSpec((D_MODEL, D_IN), lambda i: (0, 0)),
            pl.BlockSpec((1, HEAD_DIM), lambda i: (0, 0)),
            pl.BlockSpec((1, HEAD_DIM), lambda i: (0, 0)),
            rope_spec, rope_spec, rope_spec,
        ],
        out_specs=[
            pl.BlockSpec((TM, D_Q), lambda i: (i, 0)),
            pl.BlockSpec((TM, D_KV), lambda i: (i, 0)),
        ],
        out_shape=[
            jax.ShapeDtypeStruct((M_ALL, D_Q), bf16),
            jax.ShapeDtypeStruct((M_ALL, D_KV), f32),
        ],
        compiler_params=pltpu.CompilerParams(
            dimension_semantics=("arbitrary",), vmem_limit_bytes=VMEM_LIMIT),
    )(x, mod, mod, w_in, qg, kg, rope_c, rope_a, rope_b)


def _stack_heads(q_ref, first_head):
    return jnp.concatenate(
        [q_ref[:, (first_head + j) * HEAD_DIM:(first_head + j + 1) * HEAD_DIM]
         for j in range(GROUP)], axis=0)


def _sink_column(sink_ref, first, rows):
    return jnp.concatenate(
        [jnp.full((rows, 1), sink_ref[first + j], f32) for j in range(GROUP)], axis=0)


def _attend(qs, k, v, bias=None, sink=None):
    s = lax.dot_general(qs, k, (((1,), (1,)), ((), ())), preferred_element_type=f32) * SCALE
    if bias is not None:
        s = s + bias
    m = jnp.max(s, axis=-1, keepdims=True)
    if sink is not None:
        m = jnp.maximum(m, sink)
    p = jnp.exp(s - m)
    l = jnp.sum(p, axis=-1, keepdims=True)
    if sink is not None:
        l = l + jnp.exp(sink - m)
    o = jnp.dot(p.astype(bf16), v, preferred_element_type=f32)
    return o * (1.0 / l)


def _store_heads(o_ref, o, first_head, rows):
    for j in range(GROUP):
        c0 = (first_head + j) * HEAD_DIM
        o_ref[:, c0:c0 + HEAD_DIM] = o[j * rows:(j + 1) * rows].astype(o_ref.dtype)


def _ctx_attn_kernel(sink_ref, q_ref, kv_ref, o_ref, *, layer):
    for mixer in range(2):
        for g in range(2):
            kcol = mixer * 512 + g * HEAD_DIM
            k = kv_ref[:, kcol:kcol + HEAD_DIM].astype(bf16)
            v = kv_ref[:, kcol + 256:kcol + 256 + HEAD_DIM].astype(bf16)
            first = mixer * N_HEADS_A + g * GROUP
            sink = None
            if mixer == 0:
                sink = _sink_column(sink_ref, layer * N_HEADS_A + g * GROUP, SEQ)
            o = _attend(_stack_heads(q_ref, first), k, v, sink=sink)
            _store_heads(o_ref, o, first, SEQ)


def _ctx_attn(sink, q, kv, layer):
    return pl.pallas_call(
        functools.partial(_ctx_attn_kernel, layer=layer),
        grid=(BATCH,),
        in_specs=[
            pl.BlockSpec(memory_space=pltpu.SMEM),
            pl.BlockSpec((SEQ, D_Q), lambda b: (b, 0)),
            pl.BlockSpec((SEQ, D_KV), lambda b: (b, 0)),
        ],
        out_specs=pl.BlockSpec((SEQ, D_Q), lambda b: (b, 0)),
        out_shape=jax.ShapeDtypeStruct((M_CTX, D_Q), bf16),
        compiler_params=pltpu.CompilerParams(
            dimension_semantics=("arbitrary",), vmem_limit_bytes=VMEM_LIMIT),
    )(sink, q, kv)


N_QB = DEC_SEQ // BLOCK
S_B = PAST_LEN + DEC_SEQ


def _lat_attn_kernel(sink_ref, q_ref, kvp_ref, kvc_ref, kvn_ref, kvb_ref, ca_ref, cb_ref,
                     o_ref, kb_s, vb_s, *, layer):
    n = pl.program_id(1)

    @pl.when(n == 0)
    def _():
        kb_s[0:PAST_LEN, :] = cb_ref[0].astype(bf16)
        kb_s[PAST_LEN:, :] = kvb_ref[:, 0:256].astype(bf16)
        vb_s[0:PAST_LEN, :] = cb_ref[1].astype(bf16)
        vb_s[PAST_LEN:, :] = kvb_ref[:, 256:512].astype(bf16)

    a = lax.broadcasted_iota(jnp.int32, (BLOCK, 3 * BLOCK), 0)
    j = lax.broadcasted_iota(jnp.int32, (BLOCK, 3 * BLOCK), 1)
    lo = jnp.where(n > 0, 0, BLOCK)
    hi = jnp.where(n < N_QB - 1, 3 * BLOCK, 2 * BLOCK)
    ok = (j >= a) & (j <= a + 2 * BLOCK) & (j >= lo) & (j < hi)
    bias = jnp.concatenate(
        [jnp.where(ok, 0.0, NEG).astype(f32), jnp.zeros((BLOCK, PAST_LEN), f32)], axis=1)
    bias = jnp.concatenate([bias] * GROUP, axis=0)
    for g in range(N_KV_A):
        kc = slice(g * HEAD_DIM, (g + 1) * HEAD_DIM)
        vc = slice(256 + g * HEAD_DIM, 256 + (g + 1) * HEAD_DIM)
        k = jnp.concatenate(
            [kvp_ref[:, kc], kvc_ref[:, kc], kvn_ref[:, kc], ca_ref[0, :, kc]], axis=0).astype(bf16)
        v = jnp.concatenate(
            [kvp_ref[:, vc], kvc_ref[:, vc], kvn_ref[:, vc], ca_ref[1, :, kc]], axis=0).astype(bf16)
        sink = _sink_column(sink_ref, layer * N_HEADS_A + g * GROUP, BLOCK)
        o = _attend(_stack_heads(q_ref, g * GROUP), k, v, bias=bias, sink=sink)
        _store_heads(o_ref, o, g * GROUP, BLOCK)

    for g in range(N_KV_B):
        kc = slice(g * HEAD_DIM, (g + 1) * HEAD_DIM)
        first = N_HEADS_A + g * GROUP
        o = _attend(_stack_heads(q_ref, first), kb_s[:, kc], vb_s[:, kc])
        _store_heads(o_ref, o, first, BLOCK)


def _lat_attn(sink, q, kv, cache_a, cache_b, layer):
    blk0 = M_CTX // BLOCK
    cache_spec = pl.BlockSpec((None, None, 2, PAST_LEN, 256), lambda b, n: (b, layer, 0, 0, 0))
    return pl.pallas_call(
        functools.partial(_lat_attn_kernel, layer=layer),
        grid=(DEC_BATCH, N_QB),
        in_specs=[
            pl.BlockSpec(memory_space=pltpu.SMEM),
            pl.BlockSpec((BLOCK, D_Q), lambda b, n: (blk0 + b * N_QB + n, 0)),
            pl.BlockSpec((BLOCK, 512), lambda b, n: (blk0 + b * N_QB + jnp.maximum(n - 1, 0), 0)),
            pl.BlockSpec((BLOCK, 512), lambda b, n: (blk0 + b * N_QB + n, 0)),
            pl.BlockSpec((BLOCK, 512),
                         lambda b, n: (blk0 + b * N_QB + jnp.minimum(n + 1, N_QB - 1), 0)),
            pl.BlockSpec((DEC_SEQ, 512), lambda b, n: (M_CTX // DEC_SEQ + b, 1)),
            cache_spec, cache_spec,
        ],
        out_specs=pl.BlockSpec((BLOCK, D_Q), lambda b, n: (b * N_QB + n, 0)),
        out_shape=jax.ShapeDtypeStruct((M_LAT, D_Q), bf16),
        scratch_shapes=[pltpu.VMEM((S_B, 256), bf16), pltpu.VMEM((S_B, 256), bf16)],
        compiler_params=pltpu.CompilerParams(
            dimension_semantics=("arbitrary", "arbitrary"), vmem_limit_bytes=VMEM_LIMIT),
    )(sink, q, kv, kv, kv, kv, cache_a, cache_b)


def _oproj_kernel(oc_ref, ol_ref, x_ref, w_ref, gate_ref, g_ref, b_ref, shift_ref, scale_ref,
                  y_ref, h_ref):
    i = pl.program_id(0)
    o = jnp.where(i < N_CTX_TILES, oc_ref[...], ol_ref[...])
    f = jnp.dot(o, w_ref[...], preferred_element_type=f32)
    y = _ln(ALPHA * x_ref[...] + gate_ref[...] * f) * g_ref[...] + b_ref[...]
    y_ref[...] = y
    h_ref[...] = (_ln(y) * (1.0 + scale_ref[...]) + shift_ref[...]).astype(bf16)


def _oproj(o_ctx, o_lat, x, w_o, mod, ln_g, ln_b, layer):
    vec = pl.BlockSpec((1, D_MODEL), lambda i: (0, 0))
    return pl.pallas_call(
        _oproj_kernel,
        grid=(N_TILES,),
        in_specs=[
            pl.BlockSpec((TM, D_Q), lambda i: (jnp.minimum(i, N_CTX_TILES - 1), 0)),
            pl.BlockSpec((TM, D_Q), lambda i: (jnp.maximum(i - N_CTX_TILES, 0), 0)),
            pl.BlockSpec((TM, D_MODEL), lambda i: (i, 0)),
            pl.BlockSpec((D_Q, D_MODEL), lambda i: (0, 0)),
            _mod_spec(layer, 2), vec, vec, _mod_spec(layer, 3), _mod_spec(layer, 4),
        ],
        out_specs=[
            pl.BlockSpec((TM, D_MODEL), lambda i: (i, 0)),
            pl.BlockSpec((TM, D_MODEL), lambda i: (i, 0)),
        ],
        out_shape=[
            jax.ShapeDtypeStruct((M_ALL, D_MODEL), f32),
            jax.ShapeDtypeStruct((M_ALL, D_MODEL), bf16),
        ],
        compiler_params=pltpu.CompilerParams(
            dimension_semantics=("arbitrary",), vmem_limit_bytes=VMEM_LIMIT),
    )(o_ctx, o_lat, x, w_o, mod, ln_g, ln_b, mod, mod)


def _ffn_kernel(hp_ref, h_ref, hn_ref, y_ref, wv_ref, wg_ref, cwv_ref, cwg_ref, cbv_ref, cbg_ref,
                wd_ref, gate_ref, g_ref, b_ref, o_ref, hcat, acc):
    i = pl.program_id(0)
    f = pl.program_id(1)

    @pl.when(f == 0)
    def _():
        hcat[0:HALO, :] = hp_ref[...]
        hcat[HALO:HALO + TM, :] = h_ref[...]
        hcat[HALO + TM:, :] = hn_ref[...]
        acc[...] = jnp.zeros_like(acc)

    is_ctx = i < N_CTX_TILES
    seq_mask = jnp.where(is_ctx, SEQ - 1, DEC_SEQ - 1)
    base = jnp.where(is_ctx, 0, (i - N_CTX_TILES) * TM)
    pos = (lax.broadcasted_iota(jnp.int32, (TM, TF), 0) + base) & seq_mask
    first = pos == 0
    last = pos == seq_mask

    hc = hcat[...]

    def conv(w_ref, cw_ref, cb_ref):
        u = jnp.dot(hc, w_ref[...], preferred_element_type=f32)
        um = jnp.where(first, 0.0, u[HALO - 1:HALO - 1 + TM])
        up = jnp.where(last, 0.0, u[HALO + 1:HALO + 1 + TM])
        return (um * cw_ref[0:1, :] + u[HALO:HALO + TM] * cw_ref[1:2, :] + up * cw_ref[2:3, :]
                + cb_ref[...])

    val = conv(wv_ref, cwv_ref, cbv_ref)
    gate = conv(wg_ref, cwg_ref, cbg_ref)
    act = (gate * jax.nn.sigmoid(gate) * val).astype(bf16)
    acc[...] += jnp.dot(act, wd_ref[...], preferred_element_type=f32)

    @pl.when(f == N_F - 1)
    def _():
        o_ref[...] = _ln(ALPHA * y_ref[...] + gate_ref[...] * acc[...]) * g_ref[...] + b_ref[...]


def _ffn(h, y, w_up, conv_w, conv_b, w_down, mod, ln_g, ln_b, layer):
    hb = TM // HALO
    vec = pl.BlockSpec((1, D_MODEL), lambda i, f: (0, 0))
    return pl.pallas_call(
        _ffn_kernel,
        grid=(N_TILES, N_F),
        in_specs=[
            pl.BlockSpec((HALO, D_MODEL), lambda i, f: (jnp.maximum(i * hb - 1, 0), 0)),
            pl.BlockSpec((TM, D_MODEL), lambda i, f: (i, 0)),
            pl.BlockSpec((HALO, D_MODEL),
                         lambda i, f: (jnp.minimum((i + 1) * hb, M_ALL // HALO - 1), 0)),
            pl.BlockSpec((TM, D_MODEL), lambda i, f: (i, 0)),
            pl.BlockSpec((D_MODEL, TF), lambda i, f: (0, f)),
            pl.BlockSpec((D_MODEL, TF), lambda i, f: (0, N_F + f)),
            pl.BlockSpec((3, TF), lambda i, f: (0, f)),
            pl.BlockSpec((3, TF), lambda i, f: (0, N_F + f)),
            pl.BlockSpec((1, TF), lambda i, f: (0, f)),
            pl.BlockSpec((1, TF), lambda i, f: (0, N_F + f)),
            pl.BlockSpec((TF, D_MODEL), lambda i, f: (f, 0)),
            _mod_spec(layer, 5), vec, vec,
        ],
        out_specs=pl.BlockSpec((TM, D_MODEL), lambda i, f: (i, 0)),
        out_shape=jax.ShapeDtypeStruct((M_ALL, D_MODEL), f32),
        scratch_shapes=[
            pltpu.VMEM((TM + 2 * HALO, D_MODEL), bf16),
            pltpu.VMEM((TM, D_MODEL), f32),
        ],
        compiler_params=pltpu.CompilerParams(
            dimension_semantics=("arbitrary", "arbitrary"), vmem_limit_bytes=VMEM_LIMIT),
    )(h, h, h, y, w_up, w_up, conv_w, conv_w, conv_b, conv_b, w_down, mod, ln_g, ln_b)


def _rope_tables():
    pos = np.arange(DEC_SEQ)
    q4 = HEAD_DIM // 4
    freq = jnp.asarray(ROPE_THETA, f32) ** (-jnp.arange(q4, dtype=f32) / q4)
    row = jnp.asarray(pos // GRID_W, f32)[:, None] * freq
    col = jnp.asarray(pos % GRID_W, f32)[:, None] * freq
    zero = jnp.zeros_like(row)
    c = jnp.concatenate([jnp.cos(row), jnp.cos(row), jnp.cos(col), jnp.cos(col)], axis=1)
    a = jnp.concatenate([-jnp.sin(row), zero, -jnp.sin(col), zero], axis=1)
    b = jnp.concatenate([zero, jnp.sin(row), zero, jnp.sin(col)], axis=1)
    return c, a, b


def kernel(x_prompt, x_sample, cache_attn_a, cache_attn_b, c, c_ctx, w_ada, b_ada, w_in,
           q_norm_g, k_norm_g, sink_a, w_o, ln1_g, ln1_b, w_up, conv_w, conv_b, w_down,
           ln2_g, ln2_b):
    x = jnp.concatenate(
        [x_prompt.reshape(M_CTX, D_MODEL), x_sample.reshape(M_LAT, D_MODEL)], axis=0)
    cvecs = jnp.concatenate(
        [c_ctx[None], c, jnp.zeros((N_MOD_ROWS - 1 - DEC_BATCH, D_MODEL), f32)], axis=0)
    mod = _adaln(cvecs, w_ada, b_ada).reshape(DEPTH * N_MOD_ROWS * 6, 1, D_MODEL)

    w_in_b = w_in.astype(bf16)
    w_o_b = w_o.astype(bf16)
    w_up_b = w_up.astype(bf16)
    w_down_b = w_down.astype(bf16)
    rope_c, rope_a, rope_b = _rope_tables()
    sink = sink_a.reshape(DEPTH * N_HEADS_A)
    cache_a = cache_attn_a.reshape(DEC_BATCH, DEPTH, 2, PAST_LEN, N_KV_A * HEAD_DIM)
    cache_b = cache_attn_b.reshape(DEC_BATCH, DEPTH, 2, PAST_LEN, N_KV_B * HEAD_DIM)

    new_a, new_b = [], []
    for l in range(DEPTH):
        q, kv = _qkv(x, mod, w_in_b[l], q_norm_g[l][None], k_norm_g[l][None],
                     rope_c, rope_a, rope_b, l)
        o_ctx = _ctx_attn(sink, q, kv, l)
        o_lat = _lat_attn(sink, q, kv, cache_a, cache_b, l)
        y, h = _oproj(o_ctx, o_lat, x, w_o_b[l], mod, ln1_g[l][None], ln1_b[l][None], l)
        x = _ffn(h, y, w_up_b[l], conv_w[l], conv_b[l][None], w_down_b[l], mod,
                 ln2_g[l][None], ln2_b[l][None], l)
        kvc = kv[:M_CTX].reshape(BATCH, SEQ, 4, N_KV_A, HEAD_DIM)
        new_a.append(jnp.stack([kvc[:, :, 0], kvc[:, :, 1]], axis=1))
        new_b.append(jnp.stack([kvc[:, :, 2], kvc[:, :, 3]], axis=1))

    y_prompt = x[:M_CTX].reshape(BATCH, SEQ, D_MODEL)
    y_sample = x[M_CTX:].reshape(DEC_BATCH, DEC_SEQ, D_MODEL)
    return (y_prompt, y_sample, jnp.stack(new_a, axis=1), jnp.stack(new_b, axis=1))
```

```python
import functools

import jax
import jax.numpy as jnp
import numpy as np
from jax import lax
from jax.experimental import pallas as pl
from jax.experimental.pallas import tpu as pltpu

D_MODEL = 2048
BATCH = 32
SEQ = 256
DEPTH = 2
DEC_BATCH = 2
DEC_SEQ = 2048
PAST_LEN = 256
GRID_W = 64
HEAD_DIM = 128
N_HEADS_A = 8
N_KV_A = 2
N_HEADS_B = 8
N_KV_B = 2
GROUP = 4
BLOCK = 128
D_FF = 5632
ROPE_THETA = 10000.0
LN_EPS = 1e-6
ALPHA = (2.0 * DEPTH) ** 0.25
SCALE = HEAD_DIM ** -0.5
NEG = -1e30

D_Q = (N_HEADS_A + N_HEADS_B) * HEAD_DIM
D_KVH = N_KV_A * HEAD_DIM
D_KV = 4 * D_KVH
D_IN = D_Q + D_KV
OFF_QA, OFF_KA, OFF_VA = 0, 1024, 1280
OFF_QB, OFF_KB, OFF_VB = 1536, 2560, 2816

M_CTX = BATCH * SEQ
M_LAT = DEC_BATCH * DEC_SEQ
M_ALL = M_CTX + M_LAT
N_MOD_ROWS = 8

TM = 2 * SEQ
N_TILES = M_ALL // TM
N_CTX_TILES = M_CTX // TM
TILES_PER_LAT_SEQ = DEC_SEQ // TM
TM_DOWN = 256
HALO = 16
TF = 512
N_F = D_FF // TF
SUB = 256
TN_ADA = 1024
VMEM_LIMIT = 56 * 1024 * 1024

f32 = jnp.float32
bf16 = jnp.bfloat16


def _params(*sem):
    return pltpu.CompilerParams(dimension_semantics=sem, vmem_limit_bytes=VMEM_LIMIT)


def _ln(x):
    mu = jnp.mean(x, axis=-1, keepdims=True)
    xc = x - mu
    var = jnp.mean(xc * xc, axis=-1, keepdims=True)
    return xc * lax.rsqrt(var + LN_EPS)


def _mod_spec(layer, which, tm=TM, axis=0):
    n_ctx = M_CTX // tm
    per_seq = DEC_SEQ // tm

    def index_map(*idx):
        i = idx[axis]
        row = jnp.where(i < n_ctx, 0, 1 + (i - n_ctx) // per_seq)
        return ((layer * N_MOD_ROWS + row) * 6 + which, 0, 0)
    return pl.BlockSpec((None, 1, D_MODEL), index_map)


def _row_specs(split, tm=TM):
    n_ctx = M_CTX // tm
    if not split:
        return [pl.BlockSpec((tm, D_MODEL), lambda i: (i, 0))]
    return [pl.BlockSpec((tm, D_MODEL), lambda i: (jnp.minimum(i, n_ctx - 1), 0)),
            pl.BlockSpec((tm, D_MODEL), lambda i: (jnp.maximum(i - n_ctx, 0), 0))]


def _row_load(refs, is_ctx):
    if len(refs) == 1:
        return refs[0][...]
    return jnp.where(is_ctx, refs[0][...], refs[1][...])


def _adaln_kernel(cv_ref, w_ref, b_ref, o_ref):
    cv = cv_ref[...]
    a = (cv * jax.nn.sigmoid(cv)).astype(bf16)
    o_ref[...] = jnp.dot(a, w_ref[...].astype(bf16), preferred_element_type=f32) + b_ref[...]


def _adaln(cvecs, w_ada, b_ada):
    return pl.pallas_call(
        _adaln_kernel,
        grid=(DEPTH, 6 * D_MODEL // TN_ADA),
        in_specs=[
            pl.BlockSpec((N_MOD_ROWS, D_MODEL), lambda l, n: (0, 0)),
            pl.BlockSpec((None, D_MODEL, TN_ADA), lambda l, n: (l, 0, n)),
            pl.BlockSpec((None, 1, TN_ADA), lambda l, n: (l, 0, n)),
        ],
        out_specs=pl.BlockSpec((None, N_MOD_ROWS, TN_ADA), lambda l, n: (l, 0, n)),
        out_shape=jax.ShapeDtypeStruct((DEPTH, N_MOD_ROWS, 6 * D_MODEL), f32),
        compiler_params=_params("arbitrary", "arbitrary"),
    )(cvecs, w_ada, b_ada.reshape(DEPTH, 1, 6 * D_MODEL))


def _qkv_kernel(*refs, n_x, n_alias):
    x_refs = refs[:n_x]
    (shift_ref, scale_ref, w_ref, qg_ref, kg_ref, rc_ref, ra_ref, rb_ref) = refs[n_x:n_x + 8]
    q_ref, ca_ref, cb_ref, kvl_ref = refs[n_x + 8 + n_alias:]
    i = pl.program_id(0)
    is_ctx = i < N_CTX_TILES
    h = _ln(_row_load(x_refs, is_ctx)) * (1.0 + scale_ref[...]) + shift_ref[...]
    qkv = jnp.dot(h.astype(bf16), w_ref[...], preferred_element_type=f32)

    def head(col, gain, rope):
        xh = qkv[:, col:col + HEAD_DIM]
        if gain is not None:
            ms = jnp.mean(xh * xh, axis=-1, keepdims=True)
            xh = xh * lax.rsqrt(ms + LN_EPS) * gain
        if rope:
            xh = (xh * rc_ref[...] + pltpu.roll(xh, HEAD_DIM - 32, 1) * ra_ref[...]
                  + pltpu.roll(xh, 32, 1) * rb_ref[...])
        return xh

    def emit_q(rope):
        qg = qg_ref[...]
        for hh in range(N_HEADS_A):
            q_ref[:, hh * HEAD_DIM:(hh + 1) * HEAD_DIM] = head(
                OFF_QA + hh * HEAD_DIM, None, rope).astype(bf16)
        for hh in range(N_HEADS_B):
            c0 = (N_HEADS_A + hh) * HEAD_DIM
            q_ref[:, c0:c0 + HEAD_DIM] = head(OFF_QB + hh * HEAD_DIM, qg, rope).astype(bf16)

    @pl.when(is_ctx)
    def _():
        emit_q(False)
        kg = kg_ref[...]
        for s in range(TM // SEQ):
            rows = slice(s * SEQ, (s + 1) * SEQ)
            for j in range(N_KV_A):
                cols = slice(j * HEAD_DIM, (j + 1) * HEAD_DIM)
                ca_ref[s, 0, :, cols] = head(OFF_KA + j * HEAD_DIM, None, False)[rows]
                cb_ref[s, 0, :, cols] = head(OFF_KB + j * HEAD_DIM, kg, False)[rows]
            ca_ref[s, 1] = qkv[rows, OFF_VA:OFF_VA + D_KVH]
            cb_ref[s, 1] = qkv[rows, OFF_VB:OFF_VB + D_KVH]

    @pl.when(jnp.logical_not(is_ctx))
    def _():
        emit_q(True)
        kg = kg_ref[...]
        for j in range(N_KV_A):
            cols = slice(j * HEAD_DIM, (j + 1) * HEAD_DIM)
            kvl_ref[:, cols] = head(OFF_KA + j * HEAD_DIM, None, True).astype(bf16)
            colsb = slice(2 * D_KVH + j * HEAD_DIM, 2 * D_KVH + (j + 1) * HEAD_DIM)
            kvl_ref[:, colsb] = head(OFF_KB + j * HEAD_DIM, kg, True).astype(bf16)
        kvl_ref[:, D_KVH:2 * D_KVH] = qkv[:, OFF_VA:OFF_VA + D_KVH].astype(bf16)
        kvl_ref[:, 3 * D_KVH:] = qkv[:, OFF_VB:OFF_VB + D_KVH].astype(bf16)


def _qkv(xs, mod, w_in, qg, kg, rope, caches, layer):
    rope_spec = pl.BlockSpec(
        (TM, HEAD_DIM), lambda i: (jnp.maximum(i - N_CTX_TILES, 0) % TILES_PER_LAT_SEQ, 0))
    vec = pl.BlockSpec((1, HEAD_DIM), lambda i: (0, 0))
    cache_spec = pl.BlockSpec((TM // SEQ, 2, SEQ, D_KVH),
                              lambda i: (jnp.minimum(i, N_CTX_TILES - 1), layer, 0, 0))
    cache_shape = jax.ShapeDtypeStruct((BATCH, DEPTH * 2, SEQ, D_KVH), f32)
    n_in = len(xs) + 8
    return pl.pallas_call(
        functools.partial(_qkv_kernel, n_x=len(xs), n_alias=len(caches)),
        grid=(N_TILES,),
        in_specs=_row_specs(len(xs) == 2) + [
            _mod_spec(layer, 0), _mod_spec(layer, 1),
            pl.BlockSpec((D_MODEL, D_IN), lambda i: (0, 0)),
            vec, vec, rope_spec, rope_spec, rope_spec,
        ] + [pl.BlockSpec(memory_space=pl.ANY)] * len(caches),
        out_specs=[
            pl.BlockSpec((TM, D_Q), lambda i: (i, 0)),
            cache_spec, cache_spec,
            pl.BlockSpec((TM, D_KV), lambda i: (jnp.maximum(i - N_CTX_TILES, 0), 0)),
        ],
        out_shape=[
            jax.ShapeDtypeStruct((M_ALL, D_Q), bf16),
            cache_shape, cache_shape,
            jax.ShapeDtypeStruct((M_LAT, D_KV), bf16),
        ],
        input_output_aliases={n_in + k: 1 + k for k in range(len(caches))},
        compiler_params=_params("arbitrary"),
    )(*xs, mod, mod, w_in, qg, kg, *rope, *caches)


def _stack_heads(q_ref, first_head):
    return jnp.concatenate(
        [q_ref[:, (first_head + j) * HEAD_DIM:(first_head + j + 1) * HEAD_DIM]
         for j in range(GROUP)], axis=0)


def _sink_column(sink_ref, first, rows):
    return jnp.concatenate(
        [jnp.full((rows, 1), sink_ref[first + j], f32) for j in range(GROUP)], axis=0)


def _attend(qs, k, v, bias=None, sink=None):
    s = lax.dot_general(qs, k, (((1,), (1,)), ((), ())), preferred_element_type=f32) * SCALE
    if bias is not None:
        s = s + bias
    m = jnp.max(s, axis=-1, keepdims=True)
    if sink is not None:
        m = jnp.maximum(m, sink)
    p = jnp.exp(s - m)
    l = jnp.sum(p, axis=-1, keepdims=True)
    if sink is not None:
        l = l + jnp.exp(sink - m)
    o = jnp.dot(p.astype(bf16), v, preferred_element_type=f32)
    return o * (1.0 / l)


def _store_heads(o_ref, o, first_head, rows):
    for j in range(GROUP):
        c0 = (first_head + j) * HEAD_DIM
        o_ref[:, c0:c0 + HEAD_DIM] = o[j * rows:(j + 1) * rows].astype(o_ref.dtype)


def _ctx_attn_kernel(sink_ref, q_ref, ca_ref, cb_ref, o_ref, *, layer):
    for mixer, c_ref in enumerate((ca_ref, cb_ref)):
        for g in range(2):
            cols = slice(g * HEAD_DIM, (g + 1) * HEAD_DIM)
            k = c_ref[0, :, cols].astype(bf16)
            v = c_ref[1, :, cols].astype(bf16)
            first = mixer * N_HEADS_A + g * GROUP
            sink = None
            if mixer == 0:
                sink = _sink_column(sink_ref, layer * N_HEADS_A + g * GROUP, SEQ)
            o = _attend(_stack_heads(q_ref, first), k, v, sink=sink)
            _store_heads(o_ref, o, first, SEQ)


def _ctx_attn(sink, q, new_a, new_b, layer):
    cache_spec = pl.BlockSpec((None, 2, SEQ, D_KVH), lambda b: (b, layer, 0, 0))
    return pl.pallas_call(
        functools.partial(_ctx_attn_kernel, layer=layer),
        grid=(BATCH,),
        in_specs=[
            pl.BlockSpec(memory_space=pltpu.SMEM),
            pl.BlockSpec((SEQ, D_Q), lambda b: (b, 0)),
            cache_spec, cache_spec,
        ],
        out_specs=pl.BlockSpec((SEQ, D_Q), lambda b: (b, 0)),
        out_shape=jax.ShapeDtypeStruct((M_CTX, D_Q), bf16),
        compiler_params=_params("arbitrary"),
    )(sink, q, new_a, new_b)


N_QB = DEC_SEQ // BLOCK
S_B = PAST_LEN + DEC_SEQ


def _lat_attn_kernel(sink_ref, q_ref, kvp_ref, kvc_ref, kvn_ref, kvb_ref, ca_ref, cb_ref,
                     o_ref, kb_s, vb_s, *, layer):
    n = pl.program_id(1)

    @pl.when(n == 0)
    def _():
        kb_s[0:PAST_LEN, :] = cb_ref[0].astype(bf16)
        kb_s[PAST_LEN:, :] = kvb_ref[:, 0:D_KVH]
        vb_s[0:PAST_LEN, :] = cb_ref[1].astype(bf16)
        vb_s[PAST_LEN:, :] = kvb_ref[:, D_KVH:]

    a = lax.broadcasted_iota(jnp.int32, (BLOCK, 3 * BLOCK), 0)
    j = lax.broadcasted_iota(jnp.int32, (BLOCK, 3 * BLOCK), 1)
    lo = jnp.where(n > 0, 0, BLOCK)
    hi = jnp.where(n < N_QB - 1, 3 * BLOCK, 2 * BLOCK)
    ok = (j >= a) & (j <= a + 2 * BLOCK) & (j >= lo) & (j < hi)
    bias = jnp.concatenate(
        [jnp.where(ok, 0.0, NEG).astype(f32), jnp.zeros((BLOCK, PAST_LEN), f32)], axis=1)
    bias = jnp.concatenate([bias] * GROUP, axis=0)
    for g in range(N_KV_A):
        kc = slice(g * HEAD_DIM, (g + 1) * HEAD_DIM)
        vc = slice(D_KVH + g * HEAD_DIM, D_KVH + (g + 1) * HEAD_DIM)
        k = jnp.concatenate(
            [kvp_ref[:, kc], kvc_ref[:, kc], kvn_ref[:, kc], ca_ref[0, :, kc].astype(bf16)], axis=0)
        v = jnp.concatenate(
            [kvp_ref[:, vc], kvc_ref[:, vc], kvn_ref[:, vc], ca_ref[1, :, kc].astype(bf16)], axis=0)
        sink = _sink_column(sink_ref, layer * N_HEADS_A + g * GROUP, BLOCK)
        o = _attend(_stack_heads(q_ref, g * GROUP), k, v, bias=bias, sink=sink)
        _store_heads(o_ref, o, g * GROUP, BLOCK)

    for g in range(N_KV_B):
        kc = slice(g * HEAD_DIM, (g + 1) * HEAD_DIM)
        first = N_HEADS_A + g * GROUP
        o = _attend(_stack_heads(q_ref, first), kb_s[:, kc], vb_s[:, kc])
        _store_heads(o_ref, o, first, BLOCK)


def _lat_attn(sink, q, kv_lat, cache_a, cache_b, layer):
    blk0 = M_CTX // BLOCK
    cache_spec = pl.BlockSpec((None, None, 2, PAST_LEN, D_KVH), lambda b, n: (b, layer, 0, 0, 0))
    half = 2 * D_KVH
    return pl.pallas_call(
        functools.partial(_lat_attn_kernel, layer=layer),
        grid=(DEC_BATCH, N_QB),
        in_specs=[
            pl.BlockSpec(memory_space=pltpu.SMEM),
            pl.BlockSpec((BLOCK, D_Q), lambda b, n: (blk0 + b * N_QB + n, 0)),
            pl.BlockSpec((BLOCK, half), lambda b, n: (b * N_QB + jnp.maximum(n - 1, 0), 0)),
            pl.BlockSpec((BLOCK, half), lambda b, n: (b * N_QB + n, 0)),
            pl.BlockSpec((BLOCK, half), lambda b, n: (b * N_QB + jnp.minimum(n + 1, N_QB - 1), 0)),
            pl.BlockSpec((DEC_SEQ, half), lambda b, n: (b, 1)),
            cache_spec, cache_spec,
        ],
        out_specs=pl.BlockSpec((BLOCK, D_Q), lambda b, n: (b * N_QB + n, 0)),
        out_shape=jax.ShapeDtypeStruct((M_LAT, D_Q), bf16),
        scratch_shapes=[pltpu.VMEM((S_B, D_KVH), bf16), pltpu.VMEM((S_B, D_KVH), bf16)],
        compiler_params=_params("arbitrary", "arbitrary"),
    )(sink, q, kv_lat, kv_lat, kv_lat, kv_lat, cache_a, cache_b)


def _oproj_kernel(*refs, n_x):
    oc_ref, ol_ref = refs[:2]
    x_refs = refs[2:2 + n_x]
    w_ref, gate_ref, g_ref, b_ref, shift_ref, scale_ref, y_ref, h_ref = refs[2 + n_x:]
    is_ctx = pl.program_id(0) < N_CTX_TILES
    o = jnp.where(is_ctx, oc_ref[...], ol_ref[...])
    f = jnp.dot(o, w_ref[...], preferred_element_type=f32)
    y = _ln(ALPHA * _row_load(x_refs, is_ctx) + gate_ref[...] * f) * g_ref[...] + b_ref[...]
    y_ref[...] = y
    h_ref[...] = (_ln(y) * (1.0 + scale_ref[...]) + shift_ref[...]).astype(bf16)


def _oproj(o_ctx, o_lat, xs, w_o, mod, ln_g, ln_b, layer):
    vec = pl.BlockSpec((1, D_MODEL), lambda i: (0, 0))
    return pl.pallas_call(
        functools.partial(_oproj_kernel, n_x=len(xs)),
        grid=(N_TILES,),
        in_specs=[
            pl.BlockSpec((TM, D_Q), lambda i: (jnp.minimum(i, N_CTX_TILES - 1), 0)),
            pl.BlockSpec((TM, D_Q), lambda i: (jnp.maximum(i - N_CTX_TILES, 0), 0)),
        ] + _row_specs(len(xs) == 2) + [
            pl.BlockSpec((D_Q, D_MODEL), lambda i: (0, 0)),
            _mod_spec(layer, 2), vec, vec, _mod_spec(layer, 3), _mod_spec(layer, 4),
        ],
        out_specs=[
            pl.BlockSpec((TM, D_MODEL), lambda i: (i, 0)),
            pl.BlockSpec((TM, D_MODEL), lambda i: (i, 0)),
        ],
        out_shape=[
            jax.ShapeDtypeStruct((M_ALL, D_MODEL), f32),
            jax.ShapeDtypeStruct((M_ALL, D_MODEL), bf16),
        ],
        compiler_params=_params("arbitrary"),
    )(o_ctx, o_lat, *xs, w_o, mod, ln_g, ln_b, mod, mod)


def _conv_rows(u, cw, cb, is_ctx):
    um = u[HALO - 1:HALO - 1 + TM]
    u0 = u[HALO:HALO + TM]
    up = u[HALO + 1:HALO + 1 + TM]
    lo, mid, hi = um * cw[0:1], u0 * cw[1:2] + cb, up * cw[2:3]
    out = lo + mid + hi
    r0 = SEQ - 8
    rows = slice(r0, r0 + 16)
    r = lax.broadcasted_iota(jnp.int32, (16, u.shape[1]), 0) + r0
    fixed = jnp.where(r == SEQ - 1, lo[rows] + mid[rows],
                      jnp.where(r == SEQ, mid[rows] + hi[rows], out[rows]))
    return jnp.concatenate(
        [out[:r0], jnp.where(is_ctx, fixed, out[rows]), out[r0 + 16:]], axis=0)


def _ffn_up_kernel(hp_ref, h_ref, hn_ref, wv_ref, wg_ref, cwv_ref, cwg_ref, cbv_ref, cbg_ref,
                   a_ref, w_s, hcat):
    i = pl.program_id(1)

    @pl.when(i == 0)
    def _():
        w_s[:, :TF] = wv_ref[...].astype(bf16)
        w_s[:, TF:] = wg_ref[...].astype(bf16)

    is_ctx = i < N_CTX_TILES
    t = (i - N_CTX_TILES) % TILES_PER_LAT_SEQ
    zero = jnp.zeros((HALO, D_MODEL), bf16)
    hcat[0:HALO, :] = jnp.where(is_ctx | (t == 0), zero, hp_ref[...])
    hcat[HALO:HALO + TM, :] = h_ref[...]
    hcat[HALO + TM:, :] = jnp.where(is_ctx | (t == TILES_PER_LAT_SEQ - 1), zero, hn_ref[...])
    hc = hcat[...]

    for s in range(TF // SUB):
        cols = slice(s * SUB, (s + 1) * SUB)
        uv = jnp.dot(hc, w_s[:, s * SUB:(s + 1) * SUB], preferred_element_type=f32)
        ug = jnp.dot(hc, w_s[:, TF + s * SUB:TF + (s + 1) * SUB], preferred_element_type=f32)
        val = _conv_rows(uv, cwv_ref[:, cols], cbv_ref[:, cols], is_ctx)
        gate = _conv_rows(ug, cwg_ref[:, cols], cbg_ref[:, cols], is_ctx)
        a_ref[:, cols] = (gate * jax.nn.sigmoid(gate) * val).astype(bf16)


def _ffn_up(h, w_up, conv_w, conv_b):
    hb = TM // HALO
    return pl.pallas_call(
        _ffn_up_kernel,
        grid=(N_F, N_TILES),
        in_specs=[
            pl.BlockSpec((HALO, D_MODEL), lambda f, i: (jnp.maximum(i * hb - 1, 0), 0)),
            pl.BlockSpec((TM, D_MODEL), lambda f, i: (i, 0)),
            pl.BlockSpec((HALO, D_MODEL),
                         lambda f, i: (jnp.minimum((i + 1) * hb, M_ALL // HALO - 1), 0)),
            pl.BlockSpec((D_MODEL, TF), lambda f, i: (0, f)),
            pl.BlockSpec((D_MODEL, TF), lambda f, i: (0, N_F + f)),
            pl.BlockSpec((3, TF), lambda f, i: (0, f)),
            pl.BlockSpec((3, TF), lambda f, i: (0, N_F + f)),
            pl.BlockSpec((1, TF), lambda f, i: (0, f)),
            pl.BlockSpec((1, TF), lambda f, i: (0, N_F + f)),
        ],
        out_specs=pl.BlockSpec((TM, TF), lambda f, i: (i, f)),
        out_shape=jax.ShapeDtypeStruct((M_ALL, D_FF), bf16),
        scratch_shapes=[
            pltpu.VMEM((D_MODEL, 2 * TF), bf16),
            pltpu.VMEM((TM + 2 * HALO, D_MODEL), bf16),
        ],
        compiler_params=_params("arbitrary", "arbitrary"),
    )(h, h, h, w_up, w_up, conv_w, conv_w, conv_b, conv_b)


def _ffn_down_kernel(a_ref, y_ref, w_ref, gate_ref, g_ref, b_ref, *o_refs):
    f = jnp.dot(a_ref[...], w_ref[...], preferred_element_type=f32)
    out = _ln(ALPHA * y_ref[...] + gate_ref[...] * f) * g_ref[...] + b_ref[...]
    if len(o_refs) == 1:
        o_refs[0][...] = out
    else:
        is_ctx = pl.program_id(0) < M_CTX // TM_DOWN

        @pl.when(is_ctx)
        def _():
            o_refs[0][...] = out

        @pl.when(jnp.logical_not(is_ctx))
        def _():
            o_refs[1][...] = out


def _ffn_down(a, y, w_down, mod, ln_g, ln_b, layer, split_out):
    vec = pl.BlockSpec((1, D_MODEL), lambda i: (0, 0))
    if split_out:
        out_shape = [jax.ShapeDtypeStruct((M_CTX, D_MODEL), f32),
                     jax.ShapeDtypeStruct((M_LAT, D_MODEL), f32)]
    else:
        out_shape = [jax.ShapeDtypeStruct((M_ALL, D_MODEL), f32)]
    return pl.pallas_call(
        _ffn_down_kernel,
        grid=(M_ALL // TM_DOWN,),
        in_specs=[
            pl.BlockSpec((TM_DOWN, D_FF), lambda i: (i, 0)),
            pl.BlockSpec((TM_DOWN, D_MODEL), lambda i: (i, 0)),
            pl.BlockSpec((D_FF, D_MODEL), lambda i: (0, 0), pipeline_mode=pl.Buffered(1)),
            _mod_spec(layer, 5, tm=TM_DOWN), vec, vec,
        ],
        out_specs=_row_specs(split_out, tm=TM_DOWN),
        out_shape=out_shape,
        compiler_params=_params("arbitrary"),
    )(a, y, w_down, mod, ln_g, ln_b)


def _rope_tables():
    pos = np.arange(DEC_SEQ)
    q4 = HEAD_DIM // 4
    freq = jnp.asarray(ROPE_THETA, f32) ** (-jnp.arange(q4, dtype=f32) / q4)
    row = jnp.asarray(pos // GRID_W, f32)[:, None] * freq
    col = jnp.asarray(pos % GRID_W, f32)[:, None] * freq
    zero = jnp.zeros_like(row)
    c = jnp.concatenate([jnp.cos(row), jnp.cos(row), jnp.cos(col), jnp.cos(col)], axis=1)
    a = jnp.concatenate([-jnp.sin(row), zero, -jnp.sin(col), zero], axis=1)
    b = jnp.concatenate([zero, jnp.sin(row), zero, jnp.sin(col)], axis=1)
    return c, a, b


def kernel(x_prompt, x_sample, cache_attn_a, cache_attn_b, c, c_ctx, w_ada, b_ada, w_in,
           q_norm_g, k_norm_g, sink_a, w_o, ln1_g, ln1_b, w_up, conv_w, conv_b, w_down,
           ln2_g, ln2_b):
    cvecs = jnp.concatenate(
        [c_ctx[None], c, jnp.zeros((N_MOD_ROWS - 1 - DEC_BATCH, D_MODEL), f32)], axis=0)
    mod = _adaln(cvecs, w_ada, b_ada).reshape(DEPTH * N_MOD_ROWS * 6, 1, D_MODEL)

    w_in_b = w_in.astype(bf16)
    w_o_b = w_o.astype(bf16)
    w_down_b = w_down.astype(bf16)
    rope = _rope_tables()
    sink = sink_a.reshape(DEPTH * N_HEADS_A)
    cache_a = cache_attn_a.reshape(DEC_BATCH, DEPTH, 2, PAST_LEN, D_KVH)
    cache_b = cache_attn_b.reshape(DEC_BATCH, DEPTH, 2, PAST_LEN, D_KVH)

    xs = [x_prompt.reshape(M_CTX, D_MODEL), x_sample.reshape(M_LAT, D_MODEL)]
    new_caches = []
    for l in range(DEPTH):
        q, new_a, new_b, kv_lat = _qkv(xs, mod, w_in_b[l], q_norm_g[l][None], k_norm_g[l][None],
                                       rope, new_caches, l)
        new_caches = [new_a, new_b]
        o_ctx = _ctx_attn(sink, q, new_a, new_b, l)
        o_lat = _lat_attn(sink, q, kv_lat, cache_a, cache_b, l)
        y, h = _oproj(o_ctx, o_lat, xs, w_o_b[l], mod, ln1_g[l][None], ln1_b[l][None], l)
        act = _ffn_up(h, w_up[l], conv_w[l], conv_b[l][None])
        xs = _ffn_down(act, y, w_down_b[l], mod, ln2_g[l][None], ln2_b[l][None], l,
                       split_out=(l == DEPTH - 1))

    cache_shape = (BATCH, DEPTH, 2, SEQ, N_KV_A, HEAD_DIM)
    return (xs[0].reshape(BATCH, SEQ, D_MODEL), xs[1].reshape(DEC_BATCH, DEC_SEQ, D_MODEL),
            new_caches[0].reshape(cache_shape), new_caches[1].reshape(cache_shape))
```

```python
import functools

import jax
import jax.numpy as jnp
import numpy as np
from jax import lax
from jax.experimental import pallas as pl
from jax.experimental.pallas import tpu as pltpu

D_MODEL = 2048
BATCH = 32
SEQ = 256
DEPTH = 2
DEC_BATCH = 2
DEC_SEQ = 2048
PAST_LEN = 256
GRID_W = 64
HEAD_DIM = 128
N_HEADS_A = 8
N_KV_A = 2
N_HEADS_B = 8
N_KV_B = 2
GROUP = 4
BLOCK = 128
D_FF = 5632
ROPE_THETA = 10000.0
LN_EPS = 1e-6
ALPHA = (2.0 * DEPTH) ** 0.25
SCALE = HEAD_DIM ** -0.5
NEG = -1e30

D_Q = (N_HEADS_A + N_HEADS_B) * HEAD_DIM
D_KVH = N_KV_A * HEAD_DIM
D_KV = 4 * D_KVH
D_IN = D_Q + D_KV
OFF_QA, OFF_KA, OFF_VA = 0, 1024, 1280
OFF_QB, OFF_KB, OFF_VB = 1536, 2560, 2816

M_CTX = BATCH * SEQ
M_LAT = DEC_BATCH * DEC_SEQ
M_ALL = M_CTX + M_LAT
N_MOD_ROWS = 8

TM = 2 * SEQ
N_TILES = M_ALL // TM
N_CTX_TILES = M_CTX // TM
TILES_PER_LAT_SEQ = DEC_SEQ // TM
TM_DOWN = 256
HALO = 16
TF = 512
N_F = D_FF // TF
SUB = 256
TN_ADA = 1024
VMEM_LIMIT = 56 * 1024 * 1024

f32 = jnp.float32
bf16 = jnp.bfloat16


def _params(*sem, flags=None):
    return pltpu.CompilerParams(
        dimension_semantics=sem, vmem_limit_bytes=VMEM_LIMIT, flags=flags)


def _ln(x):
    mu = jnp.mean(x, axis=-1, keepdims=True)
    xc = x - mu
    var = jnp.mean(xc * xc, axis=-1, keepdims=True)
    return xc * lax.rsqrt(var + LN_EPS)


def _mod_spec(layer, which, tm=TM):
    n_ctx = M_CTX // tm
    per_seq = DEC_SEQ // tm

    def index_map(i):
        row = jnp.where(i < n_ctx, 0, 1 + (i - n_ctx) // per_seq)
        return ((layer * N_MOD_ROWS + row) * 6 + which, 0, 0)
    return pl.BlockSpec((None, 1, D_MODEL), index_map)


def _layer_vec_spec(layer, width):
    return pl.BlockSpec((None, 1, width), lambda i: (layer, 0, 0))


def _row_specs(split, tm=TM):
    n_ctx = M_CTX // tm
    if not split:
        return [pl.BlockSpec((tm, D_MODEL), lambda i: (i, 0))]
    return [pl.BlockSpec((tm, D_MODEL), lambda i: (jnp.minimum(i, n_ctx - 1), 0)),
            pl.BlockSpec((tm, D_MODEL), lambda i: (jnp.maximum(i - n_ctx, 0), 0))]


def _row_load(refs, is_ctx):
    if len(refs) == 1:
        return refs[0][...]
    return jnp.where(is_ctx, refs[0][...], refs[1][...])


def _adaln_kernel(cv_ref, w_ref, b_ref, o_ref):
    cv = cv_ref[...]
    a = (cv * jax.nn.sigmoid(cv)).astype(bf16)
    o_ref[...] = jnp.dot(a, w_ref[...].astype(bf16), preferred_element_type=f32) + b_ref[...]


def _adaln(cvecs, w_ada, b_ada):
    return pl.pallas_call(
        _adaln_kernel,
        grid=(DEPTH, 6 * D_MODEL // TN_ADA),
        in_specs=[
            pl.BlockSpec((N_MOD_ROWS, D_MODEL), lambda l, n: (0, 0)),
            pl.BlockSpec((None, D_MODEL, TN_ADA), lambda l, n: (l, 0, n)),
            pl.BlockSpec((None, 1, TN_ADA), lambda l, n: (l, 0, n)),
        ],
        out_specs=pl.BlockSpec((None, N_MOD_ROWS, TN_ADA), lambda l, n: (l, 0, n)),
        out_shape=jax.ShapeDtypeStruct((DEPTH, N_MOD_ROWS, 6 * D_MODEL), f32),
        compiler_params=_params("arbitrary", "arbitrary"),
    )(cvecs, w_ada, b_ada.reshape(DEPTH, 1, 6 * D_MODEL))


def _qkv_kernel(*refs, n_x, n_alias):
    x_refs = refs[:n_x]
    (shift_ref, scale_ref, w_ref, qg_ref, kg_ref, rc_ref, ra_ref, rb_ref) = refs[n_x:n_x + 8]
    q_ref, kv_ref, ca_ref, cb_ref = refs[n_x + 8 + n_alias:]
    i = pl.program_id(0)
    is_ctx = i < N_CTX_TILES
    h = _ln(_row_load(x_refs, is_ctx)) * (1.0 + scale_ref[...]) + shift_ref[...]
    qkv = jnp.dot(h.astype(bf16), w_ref[...], preferred_element_type=f32)

    def head(col, gain, rope):
        xh = qkv[:, col:col + HEAD_DIM]
        if gain is not None:
            ms = jnp.mean(xh * xh, axis=-1, keepdims=True)
            xh = xh * lax.rsqrt(ms + LN_EPS) * gain
        if rope:
            xh = (xh * rc_ref[...] + pltpu.roll(xh, HEAD_DIM - 32, 1) * ra_ref[...]
                  + pltpu.roll(xh, 32, 1) * rb_ref[...])
        return xh

    def emit(rope):
        qg = qg_ref[...]
        kg = kg_ref[...]
        for hh in range(N_HEADS_A):
            q_ref[:, hh * HEAD_DIM:(hh + 1) * HEAD_DIM] = head(
                OFF_QA + hh * HEAD_DIM, None, rope).astype(bf16)
        for hh in range(N_HEADS_B):
            c0 = (N_HEADS_A + hh) * HEAD_DIM
            q_ref[:, c0:c0 + HEAD_DIM] = head(OFF_QB + hh * HEAD_DIM, qg, rope).astype(bf16)
        va = qkv[:, OFF_VA:OFF_VA + D_KVH]
        vb = qkv[:, OFF_VB:OFF_VB + D_KVH]
        kv_ref[:, D_KVH:2 * D_KVH] = va.astype(bf16)
        kv_ref[:, 3 * D_KVH:] = vb.astype(bf16)
        for j in range(N_KV_A):
            cols = slice(j * HEAD_DIM, (j + 1) * HEAD_DIM)
            ka = head(OFF_KA + j * HEAD_DIM, None, rope)
            kb = head(OFF_KB + j * HEAD_DIM, kg, rope)
            kv_ref[:, cols] = ka.astype(bf16)
            kv_ref[:, 2 * D_KVH + j * HEAD_DIM:2 * D_KVH + (j + 1) * HEAD_DIM] = kb.astype(bf16)
            if not rope:
                for s in range(TM // SEQ):
                    rows = slice(s * SEQ, (s + 1) * SEQ)
                    dst = pl.ds(j, SEQ, stride=N_KV_A)
                    ca_ref[s, 0, dst, :] = ka[rows]
                    ca_ref[s, 1, dst, :] = va[rows, cols]
                    cb_ref[s, 0, dst, :] = kb[rows]
                    cb_ref[s, 1, dst, :] = vb[rows, cols]

    @pl.when(is_ctx)
    def _():
        emit(False)

    @pl.when(jnp.logical_not(is_ctx))
    def _():
        emit(True)


def _qkv(xs, mod, w_in, qg, kg, rope, caches, layer):
    rope_spec = pl.BlockSpec(
        (TM, HEAD_DIM), lambda i: (jnp.maximum(i - N_CTX_TILES, 0) % TILES_PER_LAT_SEQ, 0))
    cache_spec = pl.BlockSpec((TM // SEQ, 2, SEQ * N_KV_A, HEAD_DIM),
                              lambda i: (jnp.minimum(i, N_CTX_TILES - 1), layer, 0, 0))
    cache_shape = jax.ShapeDtypeStruct((BATCH, DEPTH * 2, SEQ * N_KV_A, HEAD_DIM), f32)
    n_in = len(xs) + 8
    return pl.pallas_call(
        functools.partial(_qkv_kernel, n_x=len(xs), n_alias=len(caches)),
        grid=(N_TILES,),
        in_specs=_row_specs(len(xs) == 2) + [
            _mod_spec(layer, 0), _mod_spec(layer, 1),
            pl.BlockSpec((None, D_MODEL, D_IN), lambda i: (layer, 0, 0)),
            _layer_vec_spec(layer, HEAD_DIM), _layer_vec_spec(layer, HEAD_DIM),
            rope_spec, rope_spec, rope_spec,
        ] + [pl.BlockSpec(memory_space=pl.ANY)] * len(caches),
        out_specs=[
            pl.BlockSpec((TM, D_Q), lambda i: (i, 0)),
            pl.BlockSpec((TM, D_KV), lambda i: (i, 0)),
            cache_spec, cache_spec,
        ],
        out_shape=[
            jax.ShapeDtypeStruct((M_ALL, D_Q), bf16),
            jax.ShapeDtypeStruct((M_ALL, D_KV), bf16),
            cache_shape, cache_shape,
        ],
        input_output_aliases={n_in + k: 2 + k for k in range(len(caches))},
        compiler_params=_params("arbitrary"),
    )(*xs, mod, mod, w_in, qg, kg, *rope, *caches)


def _stack_heads(q_ref, first_head):
    return jnp.concatenate(
        [q_ref[:, (first_head + j) * HEAD_DIM:(first_head + j + 1) * HEAD_DIM]
         for j in range(GROUP)], axis=0)


def _sink_column(sink_ref, first, rows):
    return jnp.concatenate(
        [jnp.full((rows, 1), sink_ref[first + j], f32) for j in range(GROUP)], axis=0)


def _attend(qs, k, v, bias=None, sink=None):
    s = lax.dot_general(qs, k, (((1,), (1,)), ((), ())), preferred_element_type=f32) * SCALE
    if bias is not None:
        s = s + bias
    m = jnp.max(s, axis=-1, keepdims=True)
    if sink is not None:
        m = jnp.maximum(m, sink)
    p = jnp.exp(s - m)
    l = jnp.sum(p, axis=-1, keepdims=True)
    if sink is not None:
        l = l + jnp.exp(sink - m)
    o = jnp.dot(p.astype(bf16), v, preferred_element_type=f32)
    return o * (1.0 / l)


def _store_heads(o_ref, o, first_head, rows):
    for j in range(GROUP):
        c0 = (first_head + j) * HEAD_DIM
        o_ref[:, c0:c0 + HEAD_DIM] = o[j * rows:(j + 1) * rows].astype(o_ref.dtype)


def _ctx_attn_kernel(sink_ref, q_ref, kv_ref, o_ref, *, layer):
    for mixer in range(2):
        for g in range(2):
            kcol = mixer * 2 * D_KVH + g * HEAD_DIM
            k = kv_ref[:, kcol:kcol + HEAD_DIM]
            v = kv_ref[:, kcol + D_KVH:kcol + D_KVH + HEAD_DIM]
            first = mixer * N_HEADS_A + g * GROUP
            sink = None
            if mixer == 0:
                sink = _sink_column(sink_ref, layer * N_HEADS_A + g * GROUP, SEQ)
            o = _attend(_stack_heads(q_ref, first), k, v, sink=sink)
            _store_heads(o_ref, o, first, SEQ)


def _ctx_attn(sink, q, kv, layer):
    return pl.pallas_call(
        functools.partial(_ctx_attn_kernel, layer=layer),
        grid=(BATCH,),
        in_specs=[
            pl.BlockSpec(memory_space=pltpu.SMEM),
            pl.BlockSpec((SEQ, D_Q), lambda b: (b, 0)),
            pl.BlockSpec((SEQ, D_KV), lambda b: (b, 0)),
        ],
        out_specs=pl.BlockSpec((SEQ, D_Q), lambda b: (b, 0)),
        out_shape=jax.ShapeDtypeStruct((M_CTX, D_Q), bf16),
        compiler_params=_params("arbitrary"),
    )(sink, q, kv)


N_QB = DEC_SEQ // BLOCK
S_B = PAST_LEN + DEC_SEQ


def _lat_attn_kernel(sink_ref, q_ref, kvp_ref, kvc_ref, kvn_ref, kvb_ref, ca_ref, cb_ref,
                     o_ref, kb_s, vb_s, *, layer):
    n = pl.program_id(1)

    @pl.when(n == 0)
    def _():
        kb_s[0:PAST_LEN, :] = cb_ref[0].astype(bf16)
        kb_s[PAST_LEN:, :] = kvb_ref[:, 0:D_KVH]
        vb_s[0:PAST_LEN, :] = cb_ref[1].astype(bf16)
        vb_s[PAST_LEN:, :] = kvb_ref[:, D_KVH:]

    a = lax.broadcasted_iota(jnp.int32, (BLOCK, 3 * BLOCK), 0)
    j = lax.broadcasted_iota(jnp.int32, (BLOCK, 3 * BLOCK), 1)
    lo = jnp.where(n > 0, 0, BLOCK)
    hi = jnp.where(n < N_QB - 1, 3 * BLOCK, 2 * BLOCK)
    ok = (j >= a) & (j <= a + 2 * BLOCK) & (j >= lo) & (j < hi)
    bias = jnp.concatenate(
        [jnp.where(ok, 0.0, NEG).astype(f32), jnp.zeros((BLOCK, PAST_LEN), f32)], axis=1)
    bias = jnp.concatenate([bias] * GROUP, axis=0)
    for g in range(N_KV_A):
        kc = slice(g * HEAD_DIM, (g + 1) * HEAD_DIM)
        vc = slice(D_KVH + g * HEAD_DIM, D_KVH + (g + 1) * HEAD_DIM)
        k = jnp.concatenate(
            [kvp_ref[:, kc], kvc_ref[:, kc], kvn_ref[:, kc], ca_ref[0, :, kc].astype(bf16)], axis=0)
        v = jnp.concatenate(
            [kvp_ref[:, vc], kvc_ref[:, vc], kvn_ref[:, vc], ca_ref[1, :, kc].astype(bf16)], axis=0)
        sink = _sink_column(sink_ref, layer * N_HEADS_A + g * GROUP, BLOCK)
        o = _attend(_stack_heads(q_ref, g * GROUP), k, v, bias=bias, sink=sink)
        _store_heads(o_ref, o, g * GROUP, BLOCK)

    for g in range(N_KV_B):
        kc = slice(g * HEAD_DIM, (g + 1) * HEAD_DIM)
        first = N_HEADS_A + g * GROUP
        o = _attend(_stack_heads(q_ref, first), kb_s[:, kc], vb_s[:, kc])
        _store_heads(o_ref, o, first, BLOCK)


def _lat_attn(sink, q, kv, cache_a, cache_b, layer):
    blk0 = M_CTX // BLOCK
    cache_spec = pl.BlockSpec((None, None, 2, PAST_LEN, D_KVH), lambda b, n: (b, layer, 0, 0, 0))
    half = 2 * D_KVH
    return pl.pallas_call(
        functools.partial(_lat_attn_kernel, layer=layer),
        grid=(DEC_BATCH, N_QB),
        in_specs=[
            pl.BlockSpec(memory_space=pltpu.SMEM),
            pl.BlockSpec((BLOCK, D_Q), lambda b, n: (blk0 + b * N_QB + n, 0)),
            pl.BlockSpec((BLOCK, half),
                         lambda b, n: (blk0 + b * N_QB + jnp.maximum(n - 1, 0), 0)),
            pl.BlockSpec((BLOCK, half), lambda b, n: (blk0 + b * N_QB + n, 0)),
            pl.BlockSpec((BLOCK, half),
                         lambda b, n: (blk0 + b * N_QB + jnp.minimum(n + 1, N_QB - 1), 0)),
            pl.BlockSpec((DEC_SEQ, half), lambda b, n: (M_CTX // DEC_SEQ + b, 1)),
            cache_spec, cache_spec,
        ],
        out_specs=pl.BlockSpec((BLOCK, D_Q), lambda b, n: (b * N_QB + n, 0)),
        out_shape=jax.ShapeDtypeStruct((M_LAT, D_Q), bf16),
        scratch_shapes=[pltpu.VMEM((S_B, D_KVH), bf16), pltpu.VMEM((S_B, D_KVH), bf16)],
        compiler_params=_params("arbitrary", "arbitrary"),
    )(sink, q, kv, kv, kv, kv, cache_a, cache_b)


def _oproj_kernel(*refs, n_x):
    oc_ref, ol_ref = refs[:2]
    x_refs = refs[2:2 + n_x]
    w_ref, gate_ref, g_ref, b_ref, shift_ref, scale_ref, y_ref, h_ref = refs[2 + n_x:]
    is_ctx = pl.program_id(0) < N_CTX_TILES
    o = jnp.where(is_ctx, oc_ref[...], ol_ref[...])
    f = jnp.dot(o, w_ref[...], preferred_element_type=f32)
    y = _ln(ALPHA * _row_load(x_refs, is_ctx) + gate_ref[...] * f) * g_ref[...] + b_ref[...]
    y_ref[...] = y
    h_ref[...] = (_ln(y) * (1.0 + scale_ref[...]) + shift_ref[...]).astype(bf16)


def _oproj(o_ctx, o_lat, xs, w_o, mod, ln_g, ln_b, layer):
    return pl.pallas_call(
        functools.partial(_oproj_kernel, n_x=len(xs)),
        grid=(N_TILES,),
        in_specs=[
            pl.BlockSpec((TM, D_Q), lambda i: (jnp.minimum(i, N_CTX_TILES - 1), 0)),
            pl.BlockSpec((TM, D_Q), lambda i: (jnp.maximum(i - N_CTX_TILES, 0), 0)),
        ] + _row_specs(len(xs) == 2) + [
            pl.BlockSpec((None, D_Q, D_MODEL), lambda i: (layer, 0, 0)),
            _mod_spec(layer, 2), _layer_vec_spec(layer, D_MODEL), _layer_vec_spec(layer, D_MODEL),
            _mod_spec(layer, 3), _mod_spec(layer, 4),
        ],
        out_specs=[
            pl.BlockSpec((TM, D_MODEL), lambda i: (i, 0)),
            pl.BlockSpec((TM, D_MODEL), lambda i: (i, 0)),
        ],
        out_shape=[
            jax.ShapeDtypeStruct((M_ALL, D_MODEL), f32),
            jax.ShapeDtypeStruct((M_ALL, D_MODEL), bf16),
        ],
        compiler_params=_params("arbitrary"),
    )(o_ctx, o_lat, *xs, w_o, mod, ln_g, ln_b, mod, mod)


def _conv_rows(u, cw, cb, is_ctx):
    um = pltpu.roll(u, 1, 0)[HALO:HALO + TM]
    u0 = u[HALO:HALO + TM]
    up = pltpu.roll(u, u.shape[0] - 1, 0)[HALO:HALO + TM]
    lo, mid, hi = um * cw[0:1], u0 * cw[1:2] + cb, up * cw[2:3]
    out = lo + mid + hi
    r0 = SEQ - 8
    rows = slice(r0, r0 + 16)
    r = lax.broadcasted_iota(jnp.int32, (16, u.shape[1]), 0) + r0
    fixed = jnp.where(r == SEQ - 1, lo[rows] + mid[rows],
                      jnp.where(r == SEQ, mid[rows] + hi[rows], out[rows]))
    return jnp.concatenate(
        [out[:r0], jnp.where(is_ctx, fixed, out[rows]), out[r0 + 16:]], axis=0)


TILES_PER_STEP = 2


def _ffn_up_kernel(hp_ref, h_ref, hn_ref, wv_ref, wg_ref, cwv_ref, cwg_ref, cbv_ref, cbg_ref,
                   a_ref, w_s, *hcats):
    j = pl.program_id(1)

    @pl.when(j == 0)
    def _():
        w_s[:, :TF] = wv_ref[...].astype(bf16)
        w_s[:, TF:] = wg_ref[...].astype(bf16)

    zero = jnp.zeros((HALO, D_MODEL), bf16)
    for k, hcat in enumerate(hcats):
        i = j * TILES_PER_STEP + k
        is_ctx = i < N_CTX_TILES
        t = (i - N_CTX_TILES) % TILES_PER_LAT_SEQ
        r0 = k * TM
        prev = hp_ref[...] if k == 0 else h_ref[r0 - HALO:r0, :]
        nxt = hn_ref[...] if k == TILES_PER_STEP - 1 else h_ref[r0 + TM:r0 + TM + HALO, :]
        hcat[0:HALO, :] = jnp.where(is_ctx | (t == 0), zero, prev)
        hcat[HALO:HALO + TM, :] = h_ref[r0:r0 + TM, :]
        hcat[HALO + TM:, :] = jnp.where(is_ctx | (t == TILES_PER_LAT_SEQ - 1), zero, nxt)
        hc = hcat[...]
        for s in range(TF // SUB):
            cols = slice(s * SUB, (s + 1) * SUB)
            uv = jnp.dot(hc, w_s[:, s * SUB:(s + 1) * SUB], preferred_element_type=f32)
            ug = jnp.dot(hc, w_s[:, TF + s * SUB:TF + (s + 1) * SUB], preferred_element_type=f32)
            val = _conv_rows(uv, cwv_ref[:, cols], cbv_ref[:, cols], is_ctx)
            gate = _conv_rows(ug, cwg_ref[:, cols], cbg_ref[:, cols], is_ctx)
            a_ref[r0:r0 + TM, cols] = (gate * jax.nn.sigmoid(gate) * val).astype(bf16)


def _ffn_up(h, w_up, conv_w, conv_b, layer):
    rows = TILES_PER_STEP * TM
    hb = rows // HALO
    return pl.pallas_call(
        _ffn_up_kernel,
        grid=(N_F, M_ALL // rows),
        in_specs=[
            pl.BlockSpec((HALO, D_MODEL), lambda f, j: (jnp.maximum(j * hb - 1, 0), 0)),
            pl.BlockSpec((rows, D_MODEL), lambda f, j: (j, 0)),
            pl.BlockSpec((HALO, D_MODEL),
                         lambda f, j: (jnp.minimum((j + 1) * hb, M_ALL // HALO - 1), 0)),
            pl.BlockSpec((None, D_MODEL, TF), lambda f, j: (layer, 0, f)),
            pl.BlockSpec((None, D_MODEL, TF), lambda f, j: (layer, 0, N_F + f)),
            pl.BlockSpec((None, 3, TF), lambda f, j: (layer, 0, f)),
            pl.BlockSpec((None, 3, TF), lambda f, j: (layer, 0, N_F + f)),
            pl.BlockSpec((None, 1, TF), lambda f, j: (layer, 0, f)),
            pl.BlockSpec((None, 1, TF), lambda f, j: (layer, 0, N_F + f)),
        ],
        out_specs=pl.BlockSpec((rows, TF), lambda f, j: (j, f)),
        out_shape=jax.ShapeDtypeStruct((M_ALL, D_FF), bf16),
        scratch_shapes=[pltpu.VMEM((D_MODEL, 2 * TF), bf16)] + [
            pltpu.VMEM((TM + 2 * HALO, D_MODEL), bf16) for _ in range(TILES_PER_STEP)],
        compiler_params=_params("arbitrary", "arbitrary"),
    )(h, h, h, w_up, w_up, conv_w, conv_w, conv_b, conv_b)


def _ffn_down_kernel(a_ref, y_ref, w_ref, gate_ref, g_ref, b_ref, *o_refs):
    f = jnp.dot(a_ref[...], w_ref[...], preferred_element_type=f32)
    out = _ln(ALPHA * y_ref[...] + gate_ref[...] * f) * g_ref[...] + b_ref[...]
    if len(o_refs) == 1:
        o_refs[0][...] = out
    else:
        is_ctx = pl.program_id(0) < M_CTX // TM_DOWN

        @pl.when(is_ctx)
        def _():
            o_refs[0][...] = out

        @pl.when(jnp.logical_not(is_ctx))
        def _():
            o_refs[1][...] = out


def _ffn_down(a, y, w_down, mod, ln_g, ln_b, layer, split_out):
    if split_out:
        out_shape = [jax.ShapeDtypeStruct((M_CTX, D_MODEL), f32),
                     jax.ShapeDtypeStruct((M_LAT, D_MODEL), f32)]
    else:
        out_shape = [jax.ShapeDtypeStruct((M_ALL, D_MODEL), f32)]
    return pl.pallas_call(
        _ffn_down_kernel,
        grid=(M_ALL // TM_DOWN,),
        in_specs=[
            pl.BlockSpec((TM_DOWN, D_FF), lambda i: (i, 0)),
            pl.BlockSpec((TM_DOWN, D_MODEL), lambda i: (i, 0)),
            pl.BlockSpec((None, D_FF, D_MODEL), lambda i: (layer, 0, 0),
                         pipeline_mode=pl.Buffered(1)),
            _mod_spec(layer, 5, tm=TM_DOWN),
            _layer_vec_spec(layer, D_MODEL), _layer_vec_spec(layer, D_MODEL),
        ],
        out_specs=_row_specs(split_out, tm=TM_DOWN),
        out_shape=out_shape,
        compiler_params=_params("arbitrary"),
    )(a, y, w_down, mod, ln_g, ln_b)


def _rope_tables():
    pos = np.arange(DEC_SEQ)
    q4 = HEAD_DIM // 4
    freq = jnp.asarray(ROPE_THETA, f32) ** (-jnp.arange(q4, dtype=f32) / q4)
    row = jnp.asarray(pos // GRID_W, f32)[:, None] * freq
    col = jnp.asarray(pos % GRID_W, f32)[:, None] * freq
    zero = jnp.zeros_like(row)
    c = jnp.concatenate([jnp.cos(row), jnp.cos(row), jnp.cos(col), jnp.cos(col)], axis=1)
    a = jnp.concatenate([-jnp.sin(row), zero, -jnp.sin(col), zero], axis=1)
    b = jnp.concatenate([zero, jnp.sin(row), zero, jnp.sin(col)], axis=1)
    return c, a, b


def kernel(x_prompt, x_sample, cache_attn_a, cache_attn_b, c, c_ctx, w_ada, b_ada, w_in,
           q_norm_g, k_norm_g, sink_a, w_o, ln1_g, ln1_b, w_up, conv_w, conv_b, w_down,
           ln2_g, ln2_b):
    cvecs = jnp.concatenate(
        [c_ctx[None], c, jnp.zeros((N_MOD_ROWS - 1 - DEC_BATCH, D_MODEL), f32)], axis=0)
    mod = _adaln(cvecs, w_ada, b_ada).reshape(DEPTH * N_MOD_ROWS * 6, 1, D_MODEL)

    w_in_b = w_in.astype(bf16)
    w_o_b = w_o.astype(bf16)
    w_down_b = w_down.astype(bf16)
    rope = _rope_tables()
    sink = sink_a.reshape(DEPTH * N_HEADS_A)
    cache_a = cache_attn_a.reshape(DEC_BATCH, DEPTH, 2, PAST_LEN, D_KVH)
    cache_b = cache_attn_b.reshape(DEC_BATCH, DEPTH, 2, PAST_LEN, D_KVH)
    per_layer = lambda v: v.reshape(DEPTH, 1, v.shape[-1])
    qg, kg = per_layer(q_norm_g), per_layer(k_norm_g)
    g1, b1, g2, b2 = per_layer(ln1_g), per_layer(ln1_b), per_layer(ln2_g), per_layer(ln2_b)
    conv_b3 = per_layer(conv_b)

    xs = [x_prompt.reshape(M_CTX, D_MODEL), x_sample.reshape(M_LAT, D_MODEL)]
    new_caches = []
    for l in range(DEPTH):
        q, kv, new_a, new_b = _qkv(xs, mod, w_in_b, qg, kg, rope, new_caches, l)
        new_caches = [new_a, new_b]
        o_ctx = _ctx_attn(sink, q, kv, l)
        o_lat = _lat_attn(sink, q, kv, cache_a, cache_b, l)
        y, h = _oproj(o_ctx, o_lat, xs, w_o_b, mod, g1, b1, l)
        act = _ffn_up(h, w_up, conv_w, conv_b3, l)
        xs = _ffn_down(act, y, w_down_b, mod, g2, b2, l, split_out=(l == DEPTH - 1))

    cache_shape = (BATCH, DEPTH, 2, SEQ, N_KV_A, HEAD_DIM)
    return (xs[0].reshape(BATCH, SEQ, D_MODEL), xs[1].reshape(DEC_BATCH, DEC_SEQ, D_MODEL),
            new_caches[0].reshape(cache_shape), new_caches[1].reshape(cache_shape))
```

```python
import functools

import jax
import jax.numpy as jnp
import numpy as np
from jax import lax
from jax.experimental import pallas as pl
from jax.experimental.pallas import tpu as pltpu

D_MODEL = 2048
BATCH = 32
SEQ = 256
DEPTH = 2
DEC_BATCH = 2
DEC_SEQ = 2048
PAST_LEN = 256
GRID_W = 64
HEAD_DIM = 128
N_HEADS_A = 8
N_KV_A = 2
N_HEADS_B = 8
N_KV_B = 2
GROUP = 4
BLOCK = 128
D_FF = 5632
ROPE_THETA = 10000.0
LN_EPS = 1e-6
ALPHA = (2.0 * DEPTH) ** 0.25
SCALE = HEAD_DIM ** -0.5
LOG2E = 1.4426950408889634
QSCALE = SCALE * LOG2E
NEG = -1e30

D_Q = (N_HEADS_A + N_HEADS_B) * HEAD_DIM
D_KVH = N_KV_A * HEAD_DIM
D_KV = 4 * D_KVH
D_IN = D_Q + D_KV
OFF_QA, OFF_KA, OFF_VA = 0, 1024, 1280
OFF_QB, OFF_KB, OFF_VB = 1536, 2560, 2816

M_CTX = BATCH * SEQ
M_LAT = DEC_BATCH * DEC_SEQ
M_ALL = M_CTX + M_LAT
N_MOD_ROWS = 8

TM = 2 * SEQ
N_TILES = M_ALL // TM
N_CTX_TILES = M_CTX // TM
TILES_PER_LAT_SEQ = DEC_SEQ // TM
TM_DOWN = 256
HALO = 16
TF = 512
N_F = D_FF // TF
SUB = 256
TN_ADA = 1024
VMEM_LIMIT = 56 * 1024 * 1024

f32 = jnp.float32
bf16 = jnp.bfloat16


def _params(*sem, flags=None):
    return pltpu.CompilerParams(
        dimension_semantics=sem, vmem_limit_bytes=VMEM_LIMIT, flags=flags)


def _ln(x):
    mu = jnp.mean(x, axis=-1, keepdims=True)
    xc = x - mu
    var = jnp.mean(xc * xc, axis=-1, keepdims=True)
    return xc * lax.rsqrt(var + LN_EPS)


def _mod_spec(layer, which, tm=TM):
    n_ctx = M_CTX // tm
    per_seq = DEC_SEQ // tm

    def index_map(i):
        row = jnp.where(i < n_ctx, 0, 1 + (i - n_ctx) // per_seq)
        return ((layer * N_MOD_ROWS + row) * 6 + which, 0, 0)
    return pl.BlockSpec((None, 1, D_MODEL), index_map)


def _layer_vec_spec(layer, width):
    return pl.BlockSpec((None, 1, width), lambda i: (layer, 0, 0))


def _row_specs(split, tm=TM):
    n_ctx = M_CTX // tm
    if not split:
        return [pl.BlockSpec((tm, D_MODEL), lambda i: (i, 0))]
    return [pl.BlockSpec((tm, D_MODEL), lambda i: (jnp.minimum(i, n_ctx - 1), 0)),
            pl.BlockSpec((tm, D_MODEL), lambda i: (jnp.maximum(i - n_ctx, 0), 0))]


def _row_load(refs, is_ctx):
    if len(refs) == 1:
        return refs[0][...]
    return jnp.where(is_ctx, refs[0][...], refs[1][...])


def _adaln_kernel(cv_ref, w_ref, b_ref, o_ref):
    cv = cv_ref[...]
    a = (cv * jax.nn.sigmoid(cv)).astype(bf16)
    o_ref[...] = jnp.dot(a, w_ref[...].astype(bf16), preferred_element_type=f32) + b_ref[...]


def _adaln(cvecs, w_ada, b_ada):
    return pl.pallas_call(
        _adaln_kernel,
        grid=(DEPTH, 6 * D_MODEL // TN_ADA),
        in_specs=[
            pl.BlockSpec((N_MOD_ROWS, D_MODEL), lambda l, n: (0, 0)),
            pl.BlockSpec((None, D_MODEL, TN_ADA), lambda l, n: (l, 0, n)),
            pl.BlockSpec((None, 1, TN_ADA), lambda l, n: (l, 0, n)),
        ],
        out_specs=pl.BlockSpec((None, N_MOD_ROWS, TN_ADA), lambda l, n: (l, 0, n)),
        out_shape=jax.ShapeDtypeStruct((DEPTH, N_MOD_ROWS, 6 * D_MODEL), f32),
        compiler_params=_params("arbitrary", "arbitrary"),
    )(cvecs, w_ada, b_ada.reshape(DEPTH, 1, 6 * D_MODEL))


def _qkv_kernel(*refs, n_x, n_alias):
    x_refs = refs[:n_x]
    (shift_ref, scale_ref, w_ref, qg_ref, kg_ref, rc_ref, ra_ref, rb_ref) = refs[n_x:n_x + 8]
    q_ref, kv_ref, ca_ref, cb_ref = refs[n_x + 8 + n_alias:]
    i = pl.program_id(0)
    is_ctx = i < N_CTX_TILES
    h = _ln(_row_load(x_refs, is_ctx)) * (1.0 + scale_ref[...]) + shift_ref[...]
    qkv = jnp.dot(h.astype(bf16), w_ref[...], preferred_element_type=f32)

    def head(col, gain, rope):
        xh = qkv[:, col:col + HEAD_DIM]
        if gain is not None:
            ms = jnp.mean(xh * xh, axis=-1, keepdims=True)
            xh = xh * lax.rsqrt(ms + LN_EPS) * gain
        if rope:
            xh = (xh * rc_ref[...] + pltpu.roll(xh, HEAD_DIM - 32, 1) * ra_ref[...]
                  + pltpu.roll(xh, 32, 1) * rb_ref[...])
        return xh

    def emit(rope):
        qg = qg_ref[...]
        kg = kg_ref[...]
        for hh in range(N_HEADS_A):
            q_ref[:, hh * HEAD_DIM:(hh + 1) * HEAD_DIM] = (
                head(OFF_QA + hh * HEAD_DIM, None, rope) * QSCALE).astype(bf16)
        for hh in range(N_HEADS_B):
            c0 = (N_HEADS_A + hh) * HEAD_DIM
            q_ref[:, c0:c0 + HEAD_DIM] = (
                head(OFF_QB + hh * HEAD_DIM, qg, rope) * QSCALE).astype(bf16)
        va = qkv[:, OFF_VA:OFF_VA + D_KVH]
        vb = qkv[:, OFF_VB:OFF_VB + D_KVH]
        kv_ref[:, D_KVH:2 * D_KVH] = va.astype(bf16)
        kv_ref[:, 3 * D_KVH:] = vb.astype(bf16)
        for j in range(N_KV_A):
            cols = slice(j * HEAD_DIM, (j + 1) * HEAD_DIM)
            ka = head(OFF_KA + j * HEAD_DIM, None, rope)
            kb = head(OFF_KB + j * HEAD_DIM, kg, rope)
            kv_ref[:, cols] = ka.astype(bf16)
            kv_ref[:, 2 * D_KVH + j * HEAD_DIM:2 * D_KVH + (j + 1) * HEAD_DIM] = kb.astype(bf16)
            if not rope:
                for s in range(TM // SEQ):
                    rows = slice(s * SEQ, (s + 1) * SEQ)
                    dst = pl.ds(j, SEQ, stride=N_KV_A)
                    ca_ref[s, 0, dst, :] = ka[rows]
                    ca_ref[s, 1, dst, :] = va[rows, cols]
                    cb_ref[s, 0, dst, :] = kb[rows]
                    cb_ref[s, 1, dst, :] = vb[rows, cols]

    @pl.when(is_ctx)
    def _():
        emit(False)

    @pl.when(jnp.logical_not(is_ctx))
    def _():
        emit(True)


def _qkv(xs, mod, w_in, qg, kg, rope, caches, layer):
    rope_spec = pl.BlockSpec(
        (TM, HEAD_DIM), lambda i: (jnp.maximum(i - N_CTX_TILES, 0) % TILES_PER_LAT_SEQ, 0))
    cache_spec = pl.BlockSpec((TM // SEQ, 2, SEQ * N_KV_A, HEAD_DIM),
                              lambda i: (jnp.minimum(i, N_CTX_TILES - 1), layer, 0, 0))
    cache_shape = jax.ShapeDtypeStruct((BATCH, DEPTH * 2, SEQ * N_KV_A, HEAD_DIM), f32)
    n_in = len(xs) + 8
    return pl.pallas_call(
        functools.partial(_qkv_kernel, n_x=len(xs), n_alias=len(caches)),
        grid=(N_TILES,),
        in_specs=_row_specs(len(xs) == 2) + [
            _mod_spec(layer, 0), _mod_spec(layer, 1),
            pl.BlockSpec((None, D_MODEL, D_IN), lambda i: (layer, 0, 0)),
            _layer_vec_spec(layer, HEAD_DIM), _layer_vec_spec(layer, HEAD_DIM),
            rope_spec, rope_spec, rope_spec,
        ] + [pl.BlockSpec(memory_space=pl.ANY)] * len(caches),
        out_specs=[
            pl.BlockSpec((TM, D_Q), lambda i: (i, 0)),
            pl.BlockSpec((TM, D_KV), lambda i: (i, 0)),
            cache_spec, cache_spec,
        ],
        out_shape=[
            jax.ShapeDtypeStruct((M_ALL, D_Q), bf16),
            jax.ShapeDtypeStruct((M_ALL, D_KV), bf16),
            cache_shape, cache_shape,
        ],
        input_output_aliases={n_in + k: 2 + k for k in range(len(caches))},
        compiler_params=_params("arbitrary"),
    )(*xs, mod, mod, w_in, qg, kg, *rope, *caches)


def _stack_heads(q_ref, first_head):
    return jnp.concatenate(
        [q_ref[:, (first_head + j) * HEAD_DIM:(first_head + j + 1) * HEAD_DIM]
         for j in range(GROUP)], axis=0)


def _sink_column(sink_ref, first, rows):
    return jnp.concatenate(
        [jnp.full((rows, 1), sink_ref[first + j] * LOG2E, f32) for j in range(GROUP)], axis=0)


def _logits(qs, k):
    return lax.dot_general(qs, k, (((1,), (1,)), ((), ())), preferred_element_type=f32)


def _with_ones(v):
    return jnp.concatenate([v, jnp.ones_like(v)], axis=1)


def _attend_all(jobs):
    ss = []
    for qs, k, _, bias, _ in jobs:
        s = _logits(qs, k)
        ss.append(s if bias is None else s + bias)
    ms = []
    for s, (_, _, _, _, sink) in zip(ss, jobs):
        m = jnp.max(s, axis=-1, keepdims=True)
        ms.append(m if sink is None else jnp.maximum(m, sink))
    ps = [jnp.exp2(s - m).astype(bf16) for s, m in zip(ss, ms)]
    outs = []
    for p, m, (_, _, v1, _, sink) in zip(ps, ms, jobs):
        o = jnp.dot(p, v1, preferred_element_type=f32)
        l = o[:, HEAD_DIM:]
        if sink is not None:
            l = l + jnp.exp2(sink - m)
        outs.append(o[:, :HEAD_DIM] * (1.0 / l))
    return outs


def _store_heads(o_ref, o, first_head, rows):
    for j in range(GROUP):
        c0 = (first_head + j) * HEAD_DIM
        o_ref[:, c0:c0 + HEAD_DIM] = o[j * rows:(j + 1) * rows].astype(o_ref.dtype)


def _ctx_attn_kernel(sink_ref, q_ref, kv_ref, o_ref, *, layer):
    for mixer in range(2):
        for g in range(2):
            kcol = mixer * 2 * D_KVH + g * HEAD_DIM
            k = kv_ref[:, kcol:kcol + HEAD_DIM]
            v = kv_ref[:, kcol + D_KVH:kcol + D_KVH + HEAD_DIM]
            first = mixer * N_HEADS_A + g * GROUP
            sink = None
            if mixer == 0:
                sink = _sink_column(sink_ref, layer * N_HEADS_A + g * GROUP, SEQ)
            o, = _attend_all([(_stack_heads(q_ref, first), k, _with_ones(v), None, sink)])
            _store_heads(o_ref, o, first, SEQ)


def _ctx_attn(sink, q, kv, layer):
    return pl.pallas_call(
        functools.partial(_ctx_attn_kernel, layer=layer),
        grid=(BATCH,),
        in_specs=[
            pl.BlockSpec(memory_space=pltpu.SMEM),
            pl.BlockSpec((SEQ, D_Q), lambda b: (b, 0)),
            pl.BlockSpec((SEQ, D_KV), lambda b: (b, 0)),
        ],
        out_specs=pl.BlockSpec((SEQ, D_Q), lambda b: (b, 0)),
        out_shape=jax.ShapeDtypeStruct((M_CTX, D_Q), bf16),
        compiler_params=_params("arbitrary"),
    )(sink, q, kv)


N_QB = DEC_SEQ // BLOCK
S_B = PAST_LEN + DEC_SEQ


A_PAD = BLOCK
A_ROWS = PAST_LEN + A_PAD + DEC_SEQ + A_PAD


def _window_bias():
    a = np.arange(BLOCK)[:, None]
    j = np.arange(3 * BLOCK)[None, :]
    near = (j >= a) & (j <= a + 2 * BLOCK)
    out = []
    for lo, hi in ((BLOCK, 3 * BLOCK), (0, 3 * BLOCK), (0, 2 * BLOCK)):
        ok = near & (j >= lo) & (j < hi)
        out.append(np.concatenate(
            [np.zeros((BLOCK, PAST_LEN), np.float32), np.where(ok, 0.0, NEG).astype(np.float32)],
            axis=1))
    return jnp.asarray(np.stack(out))


def _lat_attn_kernel(sink_ref, q_ref, kva_ref, kvb_ref, ca_ref, cb_ref, bias_ref,
                     o_ref, ka_s, va_s, kb_s, vb_s, *, layer):
    n = pl.program_id(1)

    @pl.when(n == 0)
    def _():
        lat_a = slice(PAST_LEN + A_PAD, PAST_LEN + A_PAD + DEC_SEQ)
        ka_s[0:PAST_LEN, :] = ca_ref[0].astype(bf16)
        ka_s[lat_a, :] = kva_ref[:, 0:D_KVH]
        kb_s[0:PAST_LEN, :] = cb_ref[0].astype(bf16)
        kb_s[PAST_LEN:, :] = kvb_ref[:, 0:D_KVH]
        for pad0 in (PAST_LEN, PAST_LEN + A_PAD + DEC_SEQ):
            ka_s[pad0:pad0 + A_PAD, :] = jnp.zeros((A_PAD, D_KVH), bf16)
            va_s[pad0:pad0 + A_PAD, :] = jnp.zeros((A_PAD, 2 * D_KVH), bf16)
        for g in range(N_KV_A):
            src = slice(g * HEAD_DIM, (g + 1) * HEAD_DIM)
            vsrc = slice(D_KVH + g * HEAD_DIM, D_KVH + (g + 1) * HEAD_DIM)
            dst = slice(2 * g * HEAD_DIM, (2 * g + 1) * HEAD_DIM)
            one = slice((2 * g + 1) * HEAD_DIM, (2 * g + 2) * HEAD_DIM)
            va_s[0:PAST_LEN, dst] = ca_ref[1, :, src].astype(bf16)
            va_s[lat_a, dst] = kva_ref[:, vsrc]
            va_s[0:PAST_LEN, one] = jnp.ones((PAST_LEN, HEAD_DIM), bf16)
            va_s[lat_a, one] = jnp.ones((DEC_SEQ, HEAD_DIM), bf16)
            vb_s[0:PAST_LEN, dst] = cb_ref[1, :, src].astype(bf16)
            vb_s[PAST_LEN:, dst] = kvb_ref[:, vsrc]
            vb_s[:, one] = jnp.ones((S_B, HEAD_DIM), bf16)

    win = pl.ds(pl.multiple_of(PAST_LEN + n * BLOCK, BLOCK), 3 * BLOCK)
    bias = jnp.concatenate([bias_ref[...]] * GROUP, axis=0)
    jobs, firsts = [], []
    for g in range(N_KV_A):
        kc = slice(g * HEAD_DIM, (g + 1) * HEAD_DIM)
        vc = slice(2 * g * HEAD_DIM, (2 * g + 2) * HEAD_DIM)
        k = jnp.concatenate([ka_s[0:PAST_LEN, kc], ka_s[win, kc]], axis=0)
        v1 = jnp.concatenate([va_s[0:PAST_LEN, vc], va_s[win, vc]], axis=0)
        sink = _sink_column(sink_ref, layer * N_HEADS_A + g * GROUP, BLOCK)
        jobs.append((_stack_heads(q_ref, g * GROUP), k, v1, bias, sink))
        firsts.append(g * GROUP)

    for g in range(N_KV_B):
        kc = slice(g * HEAD_DIM, (g + 1) * HEAD_DIM)
        first = N_HEADS_A + g * GROUP
        jobs.append((_stack_heads(q_ref, first), kb_s[:, kc],
                     vb_s[:, 2 * g * HEAD_DIM:(2 * g + 2) * HEAD_DIM], None, None))
        firsts.append(first)
    for first, o in zip(firsts, _attend_all(jobs)):
        _store_heads(o_ref, o, first, BLOCK)


def _lat_attn(sink, q, kv, cache_a, cache_b, bias, layer):
    blk0 = M_CTX // BLOCK
    cache_spec = pl.BlockSpec((None, None, 2, PAST_LEN, D_KVH), lambda b, n: (b, layer, 0, 0, 0))
    half = 2 * D_KVH
    seq0 = M_CTX // DEC_SEQ
    return pl.pallas_call(
        functools.partial(_lat_attn_kernel, layer=layer),
        grid=(DEC_BATCH, N_QB),
        in_specs=[
            pl.BlockSpec(memory_space=pltpu.SMEM),
            pl.BlockSpec((BLOCK, D_Q), lambda b, n: (blk0 + b * N_QB + n, 0)),
            pl.BlockSpec((DEC_SEQ, half), lambda b, n: (seq0 + b, 0)),
            pl.BlockSpec((DEC_SEQ, half), lambda b, n: (seq0 + b, 1)),
            cache_spec, cache_spec,
            pl.BlockSpec((None, BLOCK, PAST_LEN + 3 * BLOCK),
                         lambda b, n: (jnp.where(n == 0, 0, jnp.where(n == N_QB - 1, 2, 1)), 0, 0)),
        ],
        out_specs=pl.BlockSpec((BLOCK, D_Q), lambda b, n: (b * N_QB + n, 0)),
        out_shape=jax.ShapeDtypeStruct((M_LAT, D_Q), bf16),
        scratch_shapes=[
            pltpu.VMEM((A_ROWS, D_KVH), bf16), pltpu.VMEM((A_ROWS, 2 * D_KVH), bf16),
            pltpu.VMEM((S_B, D_KVH), bf16), pltpu.VMEM((S_B, 2 * D_KVH), bf16)],
        compiler_params=_params("arbitrary", "arbitrary"),
    )(sink, q, kv, kv, cache_a, cache_b, bias)


def _oproj_kernel(*refs, n_x):
    oc_ref, ol_ref = refs[:2]
    x_refs = refs[2:2 + n_x]
    w_ref, gate_ref, g_ref, b_ref, shift_ref, scale_ref, y_ref, h_ref = refs[2 + n_x:]
    is_ctx = pl.program_id(0) < N_CTX_TILES
    o = jnp.where(is_ctx, oc_ref[...], ol_ref[...])
    f = jnp.dot(o, w_ref[...], preferred_element_type=f32)
    y = _ln(ALPHA * _row_load(x_refs, is_ctx) + gate_ref[...] * f) * g_ref[...] + b_ref[...]
    y_ref[...] = y
    h_ref[...] = (_ln(y) * (1.0 + scale_ref[...]) + shift_ref[...]).astype(bf16)


def _oproj(o_ctx, o_lat, xs, w_o, mod, ln_g, ln_b, layer):
    return pl.pallas_call(
        functools.partial(_oproj_kernel, n_x=len(xs)),
        grid=(N_TILES,),
        in_specs=[
            pl.BlockSpec((TM, D_Q), lambda i: (jnp.minimum(i, N_CTX_TILES - 1), 0)),
            pl.BlockSpec((TM, D_Q), lambda i: (jnp.maximum(i - N_CTX_TILES, 0), 0)),
        ] + _row_specs(len(xs) == 2) + [
            pl.BlockSpec((None, D_Q, D_MODEL), lambda i: (layer, 0, 0)),
            _mod_spec(layer, 2), _layer_vec_spec(layer, D_MODEL), _layer_vec_spec(layer, D_MODEL),
            _mod_spec(layer, 3), _mod_spec(layer, 4),
        ],
        out_specs=[
            pl.BlockSpec((TM, D_MODEL), lambda i: (i, 0)),
            pl.BlockSpec((TM, D_MODEL), lambda i: (i, 0)),
        ],
        out_shape=[
            jax.ShapeDtypeStruct((M_ALL, D_MODEL), f32),
            jax.ShapeDtypeStruct((M_ALL, D_MODEL), bf16),
        ],
        compiler_params=_params("arbitrary"),
    )(o_ctx, o_lat, *xs, w_o, mod, ln_g, ln_b, mod, mod)


def _conv_rows(u, cw, cb, is_ctx):
    um = pltpu.roll(u, 1, 0)[HALO:HALO + TM]
    u0 = u[HALO:HALO + TM]
    up = pltpu.roll(u, u.shape[0] - 1, 0)[HALO:HALO + TM]
    lo, mid, hi = um * cw[0:1], u0 * cw[1:2] + cb, up * cw[2:3]
    out = lo + mid + hi
    r0 = SEQ - 8
    rows = slice(r0, r0 + 16)
    r = lax.broadcasted_iota(jnp.int32, (16, u.shape[1]), 0) + r0
    fixed = jnp.where(r == SEQ - 1, lo[rows] + mid[rows],
                      jnp.where(r == SEQ, mid[rows] + hi[rows], out[rows]))
    return jnp.concatenate(
        [out[:r0], jnp.where(is_ctx, fixed, out[rows]), out[r0 + 16:]], axis=0)


TILES_PER_STEP = 2


def _ffn_up_kernel(hp_ref, h_ref, hn_ref, wv_ref, wg_ref, cwv_ref, cwg_ref, cbv_ref, cbg_ref,
                   a_ref, w_s, *hcats):
    j = pl.program_id(1)

    @pl.when(j == 0)
    def _():
        w_s[:, :TF] = wv_ref[...].astype(bf16)
        w_s[:, TF:] = wg_ref[...].astype(bf16)

    zero = jnp.zeros((HALO, D_MODEL), bf16)
    pending = []
    for k, hcat in enumerate(hcats):
        i = j * TILES_PER_STEP + k
        is_ctx = i < N_CTX_TILES
        t = (i - N_CTX_TILES) % TILES_PER_LAT_SEQ
        r0 = k * TM
        prev = hp_ref[...] if k == 0 else h_ref[r0 - HALO:r0, :]
        nxt = hn_ref[...] if k == TILES_PER_STEP - 1 else h_ref[r0 + TM:r0 + TM + HALO, :]
        hcat[0:HALO, :] = jnp.where(is_ctx | (t == 0), zero, prev)
        hcat[HALO:HALO + TM, :] = h_ref[r0:r0 + TM, :]
        hcat[HALO + TM:, :] = jnp.where(is_ctx | (t == TILES_PER_LAT_SEQ - 1), zero, nxt)
        hc = hcat[...]
        for s in range(TF // SUB):
            uv = jnp.dot(hc, w_s[:, s * SUB:(s + 1) * SUB], preferred_element_type=f32)
            ug = jnp.dot(hc, w_s[:, TF + s * SUB:TF + (s + 1) * SUB], preferred_element_type=f32)
            pending.append((r0, s, is_ctx, uv, ug))

    for r0, s, is_ctx, uv, ug in pending:
        cols = slice(s * SUB, (s + 1) * SUB)
        val = _conv_rows(uv, cwv_ref[:, cols], cbv_ref[:, cols], is_ctx)
        gate = _conv_rows(ug, cwg_ref[:, cols], cbg_ref[:, cols], is_ctx)
        a_ref[r0:r0 + TM, cols] = (gate * jax.nn.sigmoid(gate) * val).astype(bf16)


def _ffn_up(h, w_up, conv_w, conv_b, layer):
    rows = TILES_PER_STEP * TM
    hb = rows // HALO
    return pl.pallas_call(
        _ffn_up_kernel,
        grid=(N_F, M_ALL // rows),
        in_specs=[
            pl.BlockSpec((HALO, D_MODEL), lambda f, j: (jnp.maximum(j * hb - 1, 0), 0)),
            pl.BlockSpec((rows, D_MODEL), lambda f, j: (j, 0)),
            pl.BlockSpec((HALO, D_MODEL),
                         lambda f, j: (jnp.minimum((j + 1) * hb, M_ALL // HALO - 1), 0)),
            pl.BlockSpec((None, D_MODEL, TF), lambda f, j: (layer, 0, f)),
            pl.BlockSpec((None, D_MODEL, TF), lambda f, j: (layer, 0, N_F + f)),
            pl.BlockSpec((None, 3, TF), lambda f, j: (layer, 0, f)),
            pl.BlockSpec((None, 3, TF), lambda f, j: (layer, 0, N_F + f)),
            pl.BlockSpec((None, 1, TF), lambda f, j: (layer, 0, f)),
            pl.BlockSpec((None, 1, TF), lambda f, j: (layer, 0, N_F + f)),
        ],
        out_specs=pl.BlockSpec((rows, TF), lambda f, j: (j, f)),
        out_shape=jax.ShapeDtypeStruct((M_ALL, D_FF), bf16),
        scratch_shapes=[pltpu.VMEM((D_MODEL, 2 * TF), bf16)] + [
            pltpu.VMEM((TM + 2 * HALO, D_MODEL), bf16) for _ in range(TILES_PER_STEP)],
        compiler_params=_params("arbitrary", "arbitrary"),
    )(h, h, h, w_up, w_up, conv_w, conv_w, conv_b, conv_b)


def _ffn_down_kernel(a_ref, y_ref, w_ref, gate_ref, g_ref, b_ref, *o_refs):
    f = jnp.dot(a_ref[...], w_ref[...], preferred_element_type=f32)
    out = _ln(ALPHA * y_ref[...] + gate_ref[...] * f) * g_ref[...] + b_ref[...]
    if len(o_refs) == 1:
        o_refs[0][...] = out
    else:
        is_ctx = pl.program_id(0) < M_CTX // TM_DOWN

        @pl.when(is_ctx)
        def _():
            o_refs[0][...] = out

        @pl.when(jnp.logical_not(is_ctx))
        def _():
            o_refs[1][...] = out


def _ffn_down(a, y, w_down, mod, ln_g, ln_b, layer, split_out):
    if split_out:
        out_shape = [jax.ShapeDtypeStruct((M_CTX, D_MODEL), f32),
                     jax.ShapeDtypeStruct((M_LAT, D_MODEL), f32)]
    else:
        out_shape = [jax.ShapeDtypeStruct((M_ALL, D_MODEL), f32)]
    return pl.pallas_call(
        _ffn_down_kernel,
        grid=(M_ALL // TM_DOWN,),
        in_specs=[
            pl.BlockSpec((TM_DOWN, D_FF), lambda i: (i, 0)),
            pl.BlockSpec((TM_DOWN, D_MODEL), lambda i: (i, 0)),
            pl.BlockSpec((None, D_FF, D_MODEL), lambda i: (layer, 0, 0),
                         pipeline_mode=pl.Buffered(1)),
            _mod_spec(layer, 5, tm=TM_DOWN),
            _layer_vec_spec(layer, D_MODEL), _layer_vec_spec(layer, D_MODEL),
        ],
        out_specs=_row_specs(split_out, tm=TM_DOWN),
        out_shape=out_shape,
        compiler_params=_params("arbitrary"),
    )(a, y, w_down, mod, ln_g, ln_b)


def _rope_tables():
    pos = np.arange(DEC_SEQ)
    q4 = HEAD_DIM // 4
    freq = jnp.asarray(ROPE_THETA, f32) ** (-jnp.arange(q4, dtype=f32) / q4)
    row = jnp.asarray(pos // GRID_W, f32)[:, None] * freq
    col = jnp.asarray(pos % GRID_W, f32)[:, None] * freq
    zero = jnp.zeros_like(row)
    c = jnp.concatenate([jnp.cos(row), jnp.cos(row), jnp.cos(col), jnp.cos(col)], axis=1)
    a = jnp.concatenate([-jnp.sin(row), zero, -jnp.sin(col), zero], axis=1)
    b = jnp.concatenate([zero, jnp.sin(row), zero, jnp.sin(col)], axis=1)
    return c, a, b


def kernel(x_prompt, x_sample, cache_attn_a, cache_attn_b, c, c_ctx, w_ada, b_ada, w_in,
           q_norm_g, k_norm_g, sink_a, w_o, ln1_g, ln1_b, w_up, conv_w, conv_b, w_down,
           ln2_g, ln2_b):
    cvecs = jnp.concatenate(
        [c_ctx[None], c, jnp.zeros((N_MOD_ROWS - 1 - DEC_BATCH, D_MODEL), f32)], axis=0)
    mod = _adaln(cvecs, w_ada, b_ada).reshape(DEPTH * N_MOD_ROWS * 6, 1, D_MODEL)

    w_in_b = w_in.astype(bf16)
    w_o_b = w_o.astype(bf16)
    w_down_b = w_down.astype(bf16)
    rope = _rope_tables()
    win_bias = _window_bias()
    sink = sink_a.reshape(DEPTH * N_HEADS_A)
    cache_a = cache_attn_a.reshape(DEC_BATCH, DEPTH, 2, PAST_LEN, D_KVH)
    cache_b = cache_attn_b.reshape(DEC_BATCH, DEPTH, 2, PAST_LEN, D_KVH)
    per_layer = lambda v: v.reshape(DEPTH, 1, v.shape[-1])
    qg, kg = per_layer(q_norm_g), per_layer(k_norm_g)
    g1, b1, g2, b2 = per_layer(ln1_g), per_layer(ln1_b), per_layer(ln2_g), per_layer(ln2_b)
    conv_b3 = per_layer(conv_b)

    xs = [x_prompt.reshape(M_CTX, D_MODEL), x_sample.reshape(M_LAT, D_MODEL)]
    new_caches = []
    for l in range(DEPTH):
        q, kv, new_a, new_b = _qkv(xs, mod, w_in_b, qg, kg, rope, new_caches, l)
        new_caches = [new_a, new_b]
        o_ctx = _ctx_attn(sink, q, kv, l)
        o_lat = _lat_attn(sink, q, kv, cache_a, cache_b, win_bias, l)
        y, h = _oproj(o_ctx, o_lat, xs, w_o_b, mod, g1, b1, l)
        act = _ffn_up(h, w_up, conv_w, conv_b3, l)
        xs = _ffn_down(act, y, w_down_b, mod, g2, b2, l, split_out=(l == DEPTH - 1))

    cache_shape = (BATCH, DEPTH, 2, SEQ, N_KV_A, HEAD_DIM)
    return (xs[0].reshape(BATCH, SEQ, D_MODEL), xs[1].reshape(DEC_BATCH, DEC_SEQ, D_MODEL),
            new_caches[0].reshape(cache_shape), new_caches[1].reshape(cache_shape))
```

```python
import functools

import jax
import jax.numpy as jnp
import numpy as np
from jax import lax
from jax.experimental import pallas as pl
from jax.experimental.pallas import tpu as pltpu

D_MODEL = 2048
BATCH = 32
SEQ = 256
DEPTH = 2
DEC_BATCH = 2
DEC_SEQ = 2048
PAST_LEN = 256
GRID_W = 64
HEAD_DIM = 128
N_HEADS_A = 8
N_KV_A = 2
N_HEADS_B = 8
N_KV_B = 2
GROUP = 4
BLOCK = 128
D_FF = 5632
ROPE_THETA = 10000.0
LN_EPS = 1e-6
ALPHA = (2.0 * DEPTH) ** 0.25
SCALE = HEAD_DIM ** -0.5
LOG2E = 1.4426950408889634
QSCALE = SCALE * LOG2E
NEG = -1e30

D_Q = (N_HEADS_A + N_HEADS_B) * HEAD_DIM
D_KVH = N_KV_A * HEAD_DIM
D_KV = 4 * D_KVH
D_IN = D_Q + D_KV
OFF_QA, OFF_KA, OFF_VA = 0, 1024, 1280
OFF_QB, OFF_KB, OFF_VB = 1536, 2560, 2816

M_CTX = BATCH * SEQ
M_LAT = DEC_BATCH * DEC_SEQ
M_ALL = M_CTX + M_LAT
N_MOD_ROWS = 8

TM = 2 * SEQ
N_TILES = M_ALL // TM
N_CTX_TILES = M_CTX // TM
TILES_PER_LAT_SEQ = DEC_SEQ // TM
TM_DOWN = 256
HALO = 16
TF = 512
N_F = D_FF // TF
SUB = 256
TN_ADA = 1024
VMEM_LIMIT = 60 * 1024 * 1024

f32 = jnp.float32
bf16 = jnp.bfloat16


def _params(*sem, flags=None):
    return pltpu.CompilerParams(
        dimension_semantics=sem, vmem_limit_bytes=VMEM_LIMIT, flags=flags)


def _ln(x):
    mu = jnp.mean(x, axis=-1, keepdims=True)
    xc = x - mu
    var = jnp.mean(xc * xc, axis=-1, keepdims=True)
    return xc * lax.rsqrt(var + LN_EPS)


def _mod_spec(layer, which, tm=TM):
    n_ctx = M_CTX // tm
    per_seq = DEC_SEQ // tm

    def index_map(i):
        row = jnp.where(i < n_ctx, 0, 1 + (i - n_ctx) // per_seq)
        return ((layer * N_MOD_ROWS + row) * 6 + which, 0, 0)
    return pl.BlockSpec((None, 1, D_MODEL), index_map)


def _layer_vec_spec(layer, width):
    return pl.BlockSpec((None, 1, width), lambda i: (layer, 0, 0))


def _row_specs(split, tm=TM):
    n_ctx = M_CTX // tm
    if not split:
        return [pl.BlockSpec((tm, D_MODEL), lambda i: (i, 0))]
    return [pl.BlockSpec((tm, D_MODEL), lambda i: (jnp.minimum(i, n_ctx - 1), 0)),
            pl.BlockSpec((tm, D_MODEL), lambda i: (jnp.maximum(i - n_ctx, 0), 0))]


def _row_load(refs, is_ctx):
    if len(refs) == 1:
        return refs[0][...]
    return jnp.where(is_ctx, refs[0][...], refs[1][...])


def _adaln_kernel(cv_ref, w_ref, b_ref, o_ref):
    cv = cv_ref[...]
    a = (cv * jax.nn.sigmoid(cv)).astype(bf16)
    o_ref[...] = jnp.dot(a, w_ref[...].astype(bf16), preferred_element_type=f32) + b_ref[...]


def _adaln(cvecs, w_ada, b_ada):
    return pl.pallas_call(
        _adaln_kernel,
        grid=(DEPTH, 6 * D_MODEL // TN_ADA),
        in_specs=[
            pl.BlockSpec((N_MOD_ROWS, D_MODEL), lambda l, n: (0, 0)),
            pl.BlockSpec((None, D_MODEL, TN_ADA), lambda l, n: (l, 0, n)),
            pl.BlockSpec((None, 1, TN_ADA), lambda l, n: (l, 0, n)),
        ],
        out_specs=pl.BlockSpec((None, N_MOD_ROWS, TN_ADA), lambda l, n: (l, 0, n)),
        out_shape=jax.ShapeDtypeStruct((DEPTH, N_MOD_ROWS, 6 * D_MODEL), f32),
        compiler_params=_params("arbitrary", "arbitrary"),
    )(cvecs, w_ada, b_ada.reshape(DEPTH, 1, 6 * D_MODEL))


def _qkv_kernel(*refs, n_x, n_alias):
    x_refs = refs[:n_x]
    (shift_ref, scale_ref, w_ref, qg_ref, kg_ref, rc_ref, ra_ref, rb_ref) = refs[n_x:n_x + 8]
    q_ref, kv_ref, ca_ref, cb_ref = refs[n_x + 8 + n_alias:]
    i = pl.program_id(0)
    is_ctx = i < N_CTX_TILES

    def head(qkv, rows, col, gain, rope):
        xh = qkv[:, col:col + HEAD_DIM]
        if gain is not None:
            ms = jnp.mean(xh * xh, axis=-1, keepdims=True)
            xh = xh * lax.rsqrt(ms + LN_EPS) * gain
        if rope:
            xh = (xh * rc_ref[rows, :] + pltpu.roll(xh, HEAD_DIM - 32, 1) * ra_ref[rows, :]
                  + pltpu.roll(xh, 32, 1) * rb_ref[rows, :])
        return xh

    def emit_rows(qkv, s, rope):
        rows = slice(s * SEQ, (s + 1) * SEQ)
        qg = qg_ref[...]
        kg = kg_ref[...]
        for hh in range(N_HEADS_A):
            q_ref[rows, hh * HEAD_DIM:(hh + 1) * HEAD_DIM] = (
                head(qkv, rows, OFF_QA + hh * HEAD_DIM, None, rope) * QSCALE).astype(bf16)
        for hh in range(N_HEADS_B):
            c0 = (N_HEADS_A + hh) * HEAD_DIM
            q_ref[rows, c0:c0 + HEAD_DIM] = (
                head(qkv, rows, OFF_QB + hh * HEAD_DIM, qg, rope) * QSCALE).astype(bf16)
        va = qkv[:, OFF_VA:OFF_VA + D_KVH]
        vb = qkv[:, OFF_VB:OFF_VB + D_KVH]
        kv_ref[rows, D_KVH:2 * D_KVH] = va.astype(bf16)
        kv_ref[rows, 3 * D_KVH:] = vb.astype(bf16)
        for j in range(N_KV_A):
            cols = slice(j * HEAD_DIM, (j + 1) * HEAD_DIM)
            ka = head(qkv, rows, OFF_KA + j * HEAD_DIM, None, rope)
            kb = head(qkv, rows, OFF_KB + j * HEAD_DIM, kg, rope)
            kv_ref[rows, cols] = ka.astype(bf16)
            kv_ref[rows, 2 * D_KVH + j * HEAD_DIM:2 * D_KVH + (j + 1) * HEAD_DIM] = (
                kb.astype(bf16))
            if not rope:
                dst = pl.ds(j, SEQ, stride=N_KV_A)
                ca_ref[s, 0, dst, :] = ka
                ca_ref[s, 1, dst, :] = va[:, cols]
                cb_ref[s, 0, dst, :] = kb
                cb_ref[s, 1, dst, :] = vb[:, cols]

    def emit(rope):
        hs = []
        for s in range(TM // SEQ):
            x = x_refs[-1 if rope else 0][s * SEQ:(s + 1) * SEQ, :]
            hs.append((_ln(x) * (1.0 + scale_ref[...]) + shift_ref[...]).astype(bf16))
        qkvs = [jnp.dot(h, w_ref[...], preferred_element_type=f32) for h in hs]
        for s, qkv in enumerate(qkvs):
            emit_rows(qkv, s, rope)

    @pl.when(is_ctx)
    def _():
        emit(False)
        if n_alias == 0:
            rest = ca_ref.shape[1] - 2
            for c_ref in (ca_ref, cb_ref):
                c_ref[:, 2:] = jnp.zeros((TM // SEQ, rest, SEQ * N_KV_A, HEAD_DIM), f32)

    @pl.when(jnp.logical_not(is_ctx))
    def _():
        emit(True)


def _qkv(xs, mod, w_in, qg, kg, rope, caches, layer):
    rope_spec = pl.BlockSpec(
        (TM, HEAD_DIM), lambda i: (jnp.maximum(i - N_CTX_TILES, 0) % TILES_PER_LAT_SEQ, 0))
    assert caches or layer == 0
    slots, slot0 = (2 * DEPTH, 0) if not caches else (2, layer)
    cache_spec = pl.BlockSpec((TM // SEQ, slots, SEQ * N_KV_A, HEAD_DIM),
                              lambda i: (jnp.minimum(i, N_CTX_TILES - 1), slot0, 0, 0))
    cache_shape = jax.ShapeDtypeStruct((BATCH, DEPTH * 2, SEQ * N_KV_A, HEAD_DIM), f32)
    n_in = len(xs) + 8
    return pl.pallas_call(
        functools.partial(_qkv_kernel, n_x=len(xs), n_alias=len(caches)),
        grid=(N_TILES,),
        in_specs=_row_specs(len(xs) == 2) + [
            _mod_spec(layer, 0), _mod_spec(layer, 1),
            pl.BlockSpec((None, D_MODEL, D_IN), lambda i: (layer, 0, 0),
                         pipeline_mode=pl.Buffered(1)),
            _layer_vec_spec(layer, HEAD_DIM), _layer_vec_spec(layer, HEAD_DIM),
            rope_spec, rope_spec, rope_spec,
        ] + [pl.BlockSpec(memory_space=pl.ANY)] * len(caches),
        out_specs=[
            pl.BlockSpec((TM, D_Q), lambda i: (i, 0)),
            pl.BlockSpec((TM, D_KV), lambda i: (i, 0)),
            cache_spec, cache_spec,
        ],
        out_shape=[
            jax.ShapeDtypeStruct((M_ALL, D_Q), bf16),
            jax.ShapeDtypeStruct((M_ALL, D_KV), bf16),
            cache_shape, cache_shape,
        ],
        input_output_aliases={n_in + k: 2 + k for k in range(len(caches))},
        compiler_params=_params("arbitrary"),
    )(*xs, mod, mod, w_in, qg, kg, *rope, *caches)


def _stack_heads(q_ref, first_head):
    return jnp.concatenate(
        [q_ref[:, (first_head + j) * HEAD_DIM:(first_head + j + 1) * HEAD_DIM]
         for j in range(GROUP)], axis=0)


def _sink_column(sink_ref, first, rows):
    return jnp.concatenate(
        [jnp.full((rows, 1), sink_ref[first + j] * LOG2E, f32) for j in range(GROUP)], axis=0)


def _logits(qs, k):
    return lax.dot_general(qs, k, (((1,), (1,)), ((), ())), preferred_element_type=f32)


def _with_ones(v):
    return jnp.concatenate([v, jnp.ones_like(v)], axis=1)


def _attend_all(jobs):
    ss = []
    for qs, k, _, bias, _ in jobs:
        s = _logits(qs, k)
        ss.append(s if bias is None else s + bias)
    ms = []
    for s, (_, _, _, _, sink) in zip(ss, jobs):
        m = jnp.max(s, axis=-1, keepdims=True)
        ms.append(m if sink is None else jnp.maximum(m, sink))
    ps = [jnp.exp2(s - m).astype(bf16) for s, m in zip(ss, ms)]
    outs = []
    for p, m, (_, _, v1, _, sink) in zip(ps, ms, jobs):
        o = jnp.dot(p, v1, preferred_element_type=f32)
        l = o[:, HEAD_DIM:]
        if sink is not None:
            l = l + jnp.exp2(sink - m)
        outs.append(o[:, :HEAD_DIM] * (1.0 / l))
    return outs


def _store_heads(o_ref, o, first_head, rows):
    for j in range(GROUP):
        c0 = (first_head + j) * HEAD_DIM
        o_ref[:, c0:c0 + HEAD_DIM] = o[j * rows:(j + 1) * rows].astype(o_ref.dtype)


def _ctx_attn_kernel(sink_ref, q_ref, kv_ref, o_ref, *, layer):
    for mixer in range(2):
        for g in range(2):
            kcol = mixer * 2 * D_KVH + g * HEAD_DIM
            k = kv_ref[:, kcol:kcol + HEAD_DIM]
            v = kv_ref[:, kcol + D_KVH:kcol + D_KVH + HEAD_DIM]
            first = mixer * N_HEADS_A + g * GROUP
            sink = None
            if mixer == 0:
                sink = _sink_column(sink_ref, layer * N_HEADS_A + g * GROUP, SEQ)
            o, = _attend_all([(_stack_heads(q_ref, first), k, _with_ones(v), None, sink)])
            _store_heads(o_ref, o, first, SEQ)


def _ctx_attn(sink, q, kv, layer):
    return pl.pallas_call(
        functools.partial(_ctx_attn_kernel, layer=layer),
        grid=(BATCH,),
        in_specs=[
            pl.BlockSpec(memory_space=pltpu.SMEM),
            pl.BlockSpec((SEQ, D_Q), lambda b: (b, 0)),
            pl.BlockSpec((SEQ, D_KV), lambda b: (b, 0)),
        ],
        out_specs=pl.BlockSpec((SEQ, D_Q), lambda b: (b, 0)),
        out_shape=jax.ShapeDtypeStruct((M_CTX, D_Q), bf16),
        compiler_params=_params("arbitrary"),
    )(sink, q, kv)


N_QB = DEC_SEQ // BLOCK
S_B = PAST_LEN + DEC_SEQ


A_PAD = BLOCK
A_ROWS = PAST_LEN + A_PAD + DEC_SEQ + A_PAD


def _window_bias():
    a = np.arange(BLOCK)[:, None]
    j = np.arange(3 * BLOCK)[None, :]
    near = (j >= a) & (j <= a + 2 * BLOCK)
    out = []
    for lo, hi in ((BLOCK, 3 * BLOCK), (0, 3 * BLOCK), (0, 2 * BLOCK)):
        ok = near & (j >= lo) & (j < hi)
        out.append(np.concatenate(
            [np.zeros((BLOCK, PAST_LEN), np.float32), np.where(ok, 0.0, NEG).astype(np.float32)],
            axis=1))
    return jnp.asarray(np.stack(out))


def _lat_attn_kernel(sink_ref, q_ref, kva_ref, kvb_ref, ca_ref, cb_ref, bias_ref,
                     o_ref, ka_s, va_s, kb_s, vb_s, *, layer):
    n = pl.program_id(1)

    @pl.when(n == 0)
    def _():
        lat_a = slice(PAST_LEN + A_PAD, PAST_LEN + A_PAD + DEC_SEQ)
        ka_s[0:PAST_LEN, :] = ca_ref[0].astype(bf16)
        ka_s[lat_a, :] = kva_ref[:, 0:D_KVH]
        kb_s[0:PAST_LEN, :] = cb_ref[0].astype(bf16)
        kb_s[PAST_LEN:, :] = kvb_ref[:, 0:D_KVH]
        for pad0 in (PAST_LEN, PAST_LEN + A_PAD + DEC_SEQ):
            ka_s[pad0:pad0 + A_PAD, :] = jnp.zeros((A_PAD, D_KVH), bf16)
            va_s[pad0:pad0 + A_PAD, :] = jnp.zeros((A_PAD, 2 * D_KVH), bf16)
        for g in range(N_KV_A):
            src = slice(g * HEAD_DIM, (g + 1) * HEAD_DIM)
            vsrc = slice(D_KVH + g * HEAD_DIM, D_KVH + (g + 1) * HEAD_DIM)
            dst = slice(2 * g * HEAD_DIM, (2 * g + 1) * HEAD_DIM)
            one = slice((2 * g + 1) * HEAD_DIM, (2 * g + 2) * HEAD_DIM)
            va_s[0:PAST_LEN, dst] = ca_ref[1, :, src].astype(bf16)
            va_s[lat_a, dst] = kva_ref[:, vsrc]
            va_s[0:PAST_LEN, one] = jnp.ones((PAST_LEN, HEAD_DIM), bf16)
            va_s[lat_a, one] = jnp.ones((DEC_SEQ, HEAD_DIM), bf16)
            vb_s[0:PAST_LEN, dst] = cb_ref[1, :, src].astype(bf16)
            vb_s[PAST_LEN:, dst] = kvb_ref[:, vsrc]
            vb_s[:, one] = jnp.ones((S_B, HEAD_DIM), bf16)

    win = pl.ds(pl.multiple_of(PAST_LEN + n * BLOCK, BLOCK), 3 * BLOCK)
    bias = jnp.concatenate([bias_ref[...]] * GROUP, axis=0)
    jobs, firsts = [], []
    for g in range(N_KV_A):
        kc = slice(g * HEAD_DIM, (g + 1) * HEAD_DIM)
        vc = slice(2 * g * HEAD_DIM, (2 * g + 2) * HEAD_DIM)
        k = jnp.concatenate([ka_s[0:PAST_LEN, kc], ka_s[win, kc]], axis=0)
        v1 = jnp.concatenate([va_s[0:PAST_LEN, vc], va_s[win, vc]], axis=0)
        sink = _sink_column(sink_ref, layer * N_HEADS_A + g * GROUP, BLOCK)
        jobs.append((_stack_heads(q_ref, g * GROUP), k, v1, bias, sink))
        firsts.append(g * GROUP)

    for g in range(N_KV_B):
        kc = slice(g * HEAD_DIM, (g + 1) * HEAD_DIM)
        first = N_HEADS_A + g * GROUP
        jobs.append((_stack_heads(q_ref, first), kb_s[:, kc],
                     vb_s[:, 2 * g * HEAD_DIM:(2 * g + 2) * HEAD_DIM], None, None))
        firsts.append(first)
    for first, o in zip(firsts, _attend_all(jobs)):
        _store_heads(o_ref, o, first, BLOCK)


def _lat_attn(sink, q, kv, cache_a, cache_b, bias, layer):
    blk0 = M_CTX // BLOCK
    cache_spec = pl.BlockSpec((None, None, 2, PAST_LEN, D_KVH), lambda b, n: (b, layer, 0, 0, 0))
    half = 2 * D_KVH
    seq0 = M_CTX // DEC_SEQ
    return pl.pallas_call(
        functools.partial(_lat_attn_kernel, layer=layer),
        grid=(DEC_BATCH, N_QB),
        in_specs=[
            pl.BlockSpec(memory_space=pltpu.SMEM),
            pl.BlockSpec((BLOCK, D_Q), lambda b, n: (blk0 + b * N_QB + n, 0)),
            pl.BlockSpec((DEC_SEQ, half), lambda b, n: (seq0 + b, 0)),
            pl.BlockSpec((DEC_SEQ, half), lambda b, n: (seq0 + b, 1)),
            cache_spec, cache_spec,
            pl.BlockSpec((None, BLOCK, PAST_LEN + 3 * BLOCK),
                         lambda b, n: (jnp.where(n == 0, 0, jnp.where(n == N_QB - 1, 2, 1)), 0, 0)),
        ],
        out_specs=pl.BlockSpec((BLOCK, D_Q), lambda b, n: (b * N_QB + n, 0)),
        out_shape=jax.ShapeDtypeStruct((M_LAT, D_Q), bf16),
        scratch_shapes=[
            pltpu.VMEM((A_ROWS, D_KVH), bf16), pltpu.VMEM((A_ROWS, 2 * D_KVH), bf16),
            pltpu.VMEM((S_B, D_KVH), bf16), pltpu.VMEM((S_B, 2 * D_KVH), bf16)],
        compiler_params=_params("arbitrary", "arbitrary"),
    )(sink, q, kv, kv, cache_a, cache_b, bias)


def _oproj_kernel(*refs, n_x):
    oc_ref, ol_ref = refs[:2]
    x_refs = refs[2:2 + n_x]
    w_ref, gate_ref, g_ref, b_ref, shift_ref, scale_ref, y_ref, h_ref = refs[2 + n_x:]
    is_ctx = pl.program_id(0) < N_CTX_TILES
    o = jnp.where(is_ctx, oc_ref[...], ol_ref[...])
    x = _row_load(x_refs, is_ctx)
    subs = [slice(r0, r0 + SEQ) for r0 in range(0, TM, SEQ)]
    fs = [jnp.dot(o[rows], w_ref[...], preferred_element_type=f32) for rows in subs]
    for rows, f in zip(subs, fs):
        y = _ln(ALPHA * x[rows] + gate_ref[...] * f) * g_ref[...] + b_ref[...]
        y_ref[rows, :] = y
        h_ref[rows, :] = (_ln(y) * (1.0 + scale_ref[...]) + shift_ref[...]).astype(bf16)


def _oproj(o_ctx, o_lat, xs, w_o, mod, ln_g, ln_b, layer):
    return pl.pallas_call(
        functools.partial(_oproj_kernel, n_x=len(xs)),
        grid=(N_TILES,),
        in_specs=[
            pl.BlockSpec((TM, D_Q), lambda i: (jnp.minimum(i, N_CTX_TILES - 1), 0)),
            pl.BlockSpec((TM, D_Q), lambda i: (jnp.maximum(i - N_CTX_TILES, 0), 0)),
        ] + _row_specs(len(xs) == 2) + [
            pl.BlockSpec((None, D_Q, D_MODEL), lambda i: (layer, 0, 0),
                         pipeline_mode=pl.Buffered(1)),
            _mod_spec(layer, 2), _layer_vec_spec(layer, D_MODEL), _layer_vec_spec(layer, D_MODEL),
            _mod_spec(layer, 3), _mod_spec(layer, 4),
        ],
        out_specs=[
            pl.BlockSpec((TM, D_MODEL), lambda i: (i, 0)),
            pl.BlockSpec((TM, D_MODEL), lambda i: (i, 0)),
        ],
        out_shape=[
            jax.ShapeDtypeStruct((M_ALL, D_MODEL), f32),
            jax.ShapeDtypeStruct((M_ALL, D_MODEL), bf16),
        ],
        compiler_params=_params("arbitrary"),
    )(o_ctx, o_lat, *xs, w_o, mod, ln_g, ln_b, mod, mod)


def _conv_rows(u, cw, cb, is_ctx):
    um = pltpu.roll(u, 1, 0)[HALO:HALO + TM]
    u0 = u[HALO:HALO + TM]
    up = pltpu.roll(u, u.shape[0] - 1, 0)[HALO:HALO + TM]
    lo, mid, hi = um * cw[0:1], u0 * cw[1:2] + cb, up * cw[2:3]
    out = lo + mid + hi
    r0 = SEQ - 8
    rows = slice(r0, r0 + 16)
    r = lax.broadcasted_iota(jnp.int32, (16, u.shape[1]), 0) + r0
    fixed = jnp.where(r == SEQ - 1, lo[rows] + mid[rows],
                      jnp.where(r == SEQ, mid[rows] + hi[rows], out[rows]))
    return jnp.concatenate(
        [out[:r0], jnp.where(is_ctx, fixed, out[rows]), out[r0 + 16:]], axis=0)


TILES_PER_STEP = 2


def _ffn_up_kernel(hp_ref, h_ref, hn_ref, wv_ref, wg_ref, cwv_ref, cwg_ref, cbv_ref, cbg_ref,
                   a_ref, w_s, *hcats):
    j = pl.program_id(1)

    @pl.when(j == 0)
    def _():
        w_s[:, :TF] = wv_ref[...].astype(bf16)
        w_s[:, TF:] = wg_ref[...].astype(bf16)

    zero = jnp.zeros((HALO, D_MODEL), bf16)
    pending = []
    for k, hcat in enumerate(hcats):
        i = j * TILES_PER_STEP + k
        is_ctx = i < N_CTX_TILES
        t = (i - N_CTX_TILES) % TILES_PER_LAT_SEQ
        r0 = k * TM
        prev = hp_ref[...] if k == 0 else h_ref[r0 - HALO:r0, :]
        nxt = hn_ref[...] if k == TILES_PER_STEP - 1 else h_ref[r0 + TM:r0 + TM + HALO, :]
        hcat[0:HALO, :] = jnp.where(is_ctx | (t == 0), zero, prev)
        hcat[HALO:HALO + TM, :] = h_ref[r0:r0 + TM, :]
        hcat[HALO + TM:, :] = jnp.where(is_ctx | (t == TILES_PER_LAT_SEQ - 1), zero, nxt)
        hc = hcat[...]
        for s in range(TF // SUB):
            uv = jnp.dot(hc, w_s[:, s * SUB:(s + 1) * SUB], preferred_element_type=f32)
            ug = jnp.dot(hc, w_s[:, TF + s * SUB:TF + (s + 1) * SUB], preferred_element_type=f32)
            pending.append((r0, s, is_ctx, uv, ug))

    for r0, s, is_ctx, uv, ug in pending:
        cols = slice(s * SUB, (s + 1) * SUB)
        val = _conv_rows(uv, cwv_ref[:, cols], cbv_ref[:, cols], is_ctx)
        gate = _conv_rows(ug, cwg_ref[:, cols], cbg_ref[:, cols], is_ctx)
        a_ref[r0:r0 + TM, cols] = (gate * jax.nn.sigmoid(gate) * val).astype(bf16)


def _ffn_up(h, w_up, conv_w, conv_b, layer):
    rows = TILES_PER_STEP * TM
    hb = rows // HALO
    return pl.pallas_call(
        _ffn_up_kernel,
        grid=(N_F, M_ALL // rows),
        in_specs=[
            pl.BlockSpec((HALO, D_MODEL), lambda f, j: (jnp.maximum(j * hb - 1, 0), 0)),
            pl.BlockSpec((rows, D_MODEL), lambda f, j: (j, 0)),
            pl.BlockSpec((HALO, D_MODEL),
                         lambda f, j: (jnp.minimum((j + 1) * hb, M_ALL // HALO - 1), 0)),
            pl.BlockSpec((None, D_MODEL, TF), lambda f, j: (layer, 0, f)),
            pl.BlockSpec((None, D_MODEL, TF), lambda f, j: (layer, 0, N_F + f)),
            pl.BlockSpec((None, 3, TF), lambda f, j: (layer, 0, f)),
            pl.BlockSpec((None, 3, TF), lambda f, j: (layer, 0, N_F + f)),
            pl.BlockSpec((None, 1, TF), lambda f, j: (layer, 0, f)),
            pl.BlockSpec((None, 1, TF), lambda f, j: (layer, 0, N_F + f)),
        ],
        out_specs=pl.BlockSpec((rows, TF), lambda f, j: (j, f)),
        out_shape=jax.ShapeDtypeStruct((M_ALL, D_FF), bf16),
        scratch_shapes=[pltpu.VMEM((D_MODEL, 2 * TF), bf16)] + [
            pltpu.VMEM((TM + 2 * HALO, D_MODEL), bf16) for _ in range(TILES_PER_STEP)],
        compiler_params=_params("arbitrary", "arbitrary"),
    )(h, h, h, w_up, w_up, conv_w, conv_w, conv_b, conv_b)


def _ffn_down_kernel(a_ref, y_ref, w_ref, gate_ref, g_ref, b_ref, *o_refs):
    f = jnp.dot(a_ref[...], w_ref[...], preferred_element_type=f32)
    out = _ln(ALPHA * y_ref[...] + gate_ref[...] * f) * g_ref[...] + b_ref[...]
    if len(o_refs) == 1:
        o_refs[0][...] = out
    else:
        is_ctx = pl.program_id(0) < M_CTX // TM_DOWN

        @pl.when(is_ctx)
        def _():
            o_refs[0][...] = out

        @pl.when(jnp.logical_not(is_ctx))
        def _():
            o_refs[1][...] = out


def _ffn_down(a, y, w_down, mod, ln_g, ln_b, layer, split_out):
    if split_out:
        out_shape = [jax.ShapeDtypeStruct((M_CTX, D_MODEL), f32),
                     jax.ShapeDtypeStruct((M_LAT, D_MODEL), f32)]
    else:
        out_shape = [jax.ShapeDtypeStruct((M_ALL, D_MODEL), f32)]
    return pl.pallas_call(
        _ffn_down_kernel,
        grid=(M_ALL // TM_DOWN,),
        in_specs=[
            pl.BlockSpec((TM_DOWN, D_FF), lambda i: (i, 0)),
            pl.BlockSpec((TM_DOWN, D_MODEL), lambda i: (i, 0)),
            pl.BlockSpec((None, D_FF, D_MODEL), lambda i: (layer, 0, 0),
                         pipeline_mode=pl.Buffered(1)),
            _mod_spec(layer, 5, tm=TM_DOWN),
            _layer_vec_spec(layer, D_MODEL), _layer_vec_spec(layer, D_MODEL),
        ],
        out_specs=_row_specs(split_out, tm=TM_DOWN),
        out_shape=out_shape,
        compiler_params=_params("arbitrary"),
    )(a, y, w_down, mod, ln_g, ln_b)


def _rope_tables():
    pos = np.arange(DEC_SEQ)
    q4 = HEAD_DIM // 4
    freq = jnp.asarray(ROPE_THETA, f32) ** (-jnp.arange(q4, dtype=f32) / q4)
    row = jnp.asarray(pos // GRID_W, f32)[:, None] * freq
    col = jnp.asarray(pos % GRID_W, f32)[:, None] * freq
    zero = jnp.zeros_like(row)
    c = jnp.concatenate([jnp.cos(row), jnp.cos(row), jnp.cos(col), jnp.cos(col)], axis=1)
    a = jnp.concatenate([-jnp.sin(row), zero, -jnp.sin(col), zero], axis=1)
    b = jnp.concatenate([zero, jnp.sin(row), zero, jnp.sin(col)], axis=1)
    return c, a, b


def kernel(x_prompt, x_sample, cache_attn_a, cache_attn_b, c, c_ctx, w_ada, b_ada, w_in,
           q_norm_g, k_norm_g, sink_a, w_o, ln1_g, ln1_b, w_up, conv_w, conv_b, w_down,
           ln2_g, ln2_b):
    cvecs = jnp.concatenate(
        [c_ctx[None], c, jnp.zeros((N_MOD_ROWS - 1 - DEC_BATCH, D_MODEL), f32)], axis=0)
    mod = _adaln(cvecs, w_ada, b_ada).reshape(DEPTH * N_MOD_ROWS * 6, 1, D_MODEL)

    w_in_b = w_in.astype(bf16)
    w_o_b = w_o.astype(bf16)
    w_down_b = w_down.astype(bf16)
    rope = _rope_tables()
    win_bias = _window_bias()
    sink = sink_a.reshape(DEPTH * N_HEADS_A)
    cache_a = cache_attn_a.reshape(DEC_BATCH, DEPTH, 2, PAST_LEN, D_KVH)
    cache_b = cache_attn_b.reshape(DEC_BATCH, DEPTH, 2, PAST_LEN, D_KVH)
    per_layer = lambda v: v.reshape(DEPTH, 1, v.shape[-1])
    qg, kg = per_layer(q_norm_g), per_layer(k_norm_g)
    g1, b1, g2, b2 = per_layer(ln1_g), per_layer(ln1_b), per_layer(ln2_g), per_layer(ln2_b)
    conv_b3 = per_layer(conv_b)

    xs = [x_prompt.reshape(M_CTX, D_MODEL), x_sample.reshape(M_LAT, D_MODEL)]
    new_caches = []
    for l in range(DEPTH):
        q, kv, new_a, new_b = _qkv(xs, mod, w_in_b, qg, kg, rope, new_caches, l)
        new_caches = [new_a, new_b]
        o_ctx = _ctx_attn(sink, q, kv, l)
        o_lat = _lat_attn(sink, q, kv, cache_a, cache_b, win_bias, l)
        y, h = _oproj(o_ctx, o_lat, xs, w_o_b, mod, g1, b1, l)
        act = _ffn_up(h, w_up, conv_w, conv_b3, l)
        xs = _ffn_down(act, y, w_down_b, mod, g2, b2, l, split_out=(l == DEPTH - 1))

    cache_shape = (BATCH, DEPTH, 2, SEQ, N_KV_A, HEAD_DIM)
    return (xs[0].reshape(BATCH, SEQ, D_MODEL), xs[1].reshape(DEC_BATCH, DEC_SEQ, D_MODEL),
            new_caches[0].reshape(cache_shape), new_caches[1].reshape(cache_shape))
```

```python
import functools

import jax
import jax.numpy as jnp
import numpy as np
from jax import lax
from jax.experimental import pallas as pl
from jax.experimental.pallas import tpu as pltpu

D_MODEL = 2048
BATCH = 32
SEQ = 256
DEPTH = 2
DEC_BATCH = 2
DEC_SEQ = 2048
PAST_LEN = 256
GRID_W = 64
HEAD_DIM = 128
N_HEADS_A = 8
N_KV_A = 2
N_HEADS_B = 8
N_KV_B = 2
GROUP = 4
BLOCK = 128
D_FF = 5632
ROPE_THETA = 10000.0
LN_EPS = 1e-6
ALPHA = (2.0 * DEPTH) ** 0.25
SCALE = HEAD_DIM ** -0.5
LOG2E = 1.4426950408889634
QSCALE = SCALE * LOG2E
NEG = -1e30

D_Q = (N_HEADS_A + N_HEADS_B) * HEAD_DIM
D_KVH = N_KV_A * HEAD_DIM
D_KV = 4 * D_KVH
D_IN = D_Q + D_KV
OFF_QA, OFF_KA, OFF_VA = 0, 1024, 1280
OFF_QB, OFF_KB, OFF_VB = 1536, 2560, 2816

M_CTX = BATCH * SEQ
M_LAT = DEC_BATCH * DEC_SEQ
M_ALL = M_CTX + M_LAT
N_MOD_ROWS = 8

TM = 2 * SEQ
N_TILES = M_ALL // TM
N_CTX_TILES = M_CTX // TM
TILES_PER_LAT_SEQ = DEC_SEQ // TM
TM_DOWN = 256
HALO = 16
TF = 512
N_F = D_FF // TF
TN_ADA = 1024
VMEM_LIMIT = 60 * 1024 * 1024

f32 = jnp.float32
bf16 = jnp.bfloat16


def _params(*sem, flags=None):
    return pltpu.CompilerParams(
        dimension_semantics=sem, vmem_limit_bytes=VMEM_LIMIT, flags=flags)


def _ln(x):
    mu = jnp.mean(x, axis=-1, keepdims=True)
    xc = x - mu
    var = jnp.mean(xc * xc, axis=-1, keepdims=True)
    return xc * lax.rsqrt(var + LN_EPS)


def _mod_spec(layer, which, tm=TM):
    n_ctx = M_CTX // tm
    per_seq = DEC_SEQ // tm

    def index_map(i):
        row = jnp.where(i < n_ctx, 0, 1 + (i - n_ctx) // per_seq)
        return ((layer * N_MOD_ROWS + row) * 6 + which, 0, 0)
    return pl.BlockSpec((None, 1, D_MODEL), index_map)


def _layer_vec_spec(layer, width):
    return pl.BlockSpec((None, 1, width), lambda i: (layer, 0, 0))


def _row_specs(split, tm=TM):
    n_ctx = M_CTX // tm
    if not split:
        return [pl.BlockSpec((tm, D_MODEL), lambda i: (i, 0))]
    return [pl.BlockSpec((tm, D_MODEL), lambda i: (jnp.minimum(i, n_ctx - 1), 0)),
            pl.BlockSpec((tm, D_MODEL), lambda i: (jnp.maximum(i - n_ctx, 0), 0))]


def _row_load(refs, is_ctx):
    if len(refs) == 1:
        return refs[0][...]
    return jnp.where(is_ctx, refs[0][...], refs[1][...])


def _adaln_kernel(cv_ref, w_ref, b_ref, o_ref):
    cv = cv_ref[...]
    a = (cv * jax.nn.sigmoid(cv)).astype(bf16)
    o_ref[...] = jnp.dot(a, w_ref[...].astype(bf16), preferred_element_type=f32) + b_ref[...]


def _adaln(cvecs, w_ada, b_ada):
    return pl.pallas_call(
        _adaln_kernel,
        grid=(DEPTH, 6 * D_MODEL // TN_ADA),
        in_specs=[
            pl.BlockSpec((N_MOD_ROWS, D_MODEL), lambda l, n: (0, 0)),
            pl.BlockSpec((None, D_MODEL, TN_ADA), lambda l, n: (l, 0, n)),
            pl.BlockSpec((None, 1, TN_ADA), lambda l, n: (l, 0, n)),
        ],
        out_specs=pl.BlockSpec((None, N_MOD_ROWS, TN_ADA), lambda l, n: (l, 0, n)),
        out_shape=jax.ShapeDtypeStruct((DEPTH, N_MOD_ROWS, 6 * D_MODEL), f32),
        compiler_params=_params("arbitrary", "arbitrary"),
    )(cvecs, w_ada, b_ada.reshape(DEPTH, 1, 6 * D_MODEL))


def _qkv_kernel(*refs, n_x, n_alias):
    x_refs = refs[:n_x]
    (shift_ref, scale_ref, w_ref, qg_ref, kg_ref, rc_ref, ra_ref, rb_ref) = refs[n_x:n_x + 8]
    q_ref, kv_ref, ca_ref, cb_ref = refs[n_x + 8 + n_alias:]
    i = pl.program_id(0)
    is_ctx = i < N_CTX_TILES

    def head(qkv, rows, col, gain, rope):
        xh = qkv[:, col:col + HEAD_DIM]
        if gain is not None:
            ms = jnp.mean(xh * xh, axis=-1, keepdims=True)
            xh = xh * lax.rsqrt(ms + LN_EPS) * gain
        if rope:
            xh = (xh * rc_ref[rows, :] + pltpu.roll(xh, HEAD_DIM - 32, 1) * ra_ref[rows, :]
                  + pltpu.roll(xh, 32, 1) * rb_ref[rows, :])
        return xh

    def emit_rows(qkv, s, rope):
        rows = slice(s * SEQ, (s + 1) * SEQ)
        qg = qg_ref[...]
        kg = kg_ref[...]
        for hh in range(N_HEADS_A):
            q_ref[rows, hh * HEAD_DIM:(hh + 1) * HEAD_DIM] = (
                head(qkv, rows, OFF_QA + hh * HEAD_DIM, None, rope) * QSCALE).astype(bf16)
        for hh in range(N_HEADS_B):
            c0 = (N_HEADS_A + hh) * HEAD_DIM
            q_ref[rows, c0:c0 + HEAD_DIM] = (
                head(qkv, rows, OFF_QB + hh * HEAD_DIM, qg, rope) * QSCALE).astype(bf16)
        va = qkv[:, OFF_VA:OFF_VA + D_KVH]
        vb = qkv[:, OFF_VB:OFF_VB + D_KVH]
        kv_ref[rows, D_KVH:2 * D_KVH] = va.astype(bf16)
        kv_ref[rows, 3 * D_KVH:] = vb.astype(bf16)
        for j in range(N_KV_A):
            cols = slice(j * HEAD_DIM, (j + 1) * HEAD_DIM)
            ka = head(qkv, rows, OFF_KA + j * HEAD_DIM, None, rope)
            kb = head(qkv, rows, OFF_KB + j * HEAD_DIM, kg, rope)
            kv_ref[rows, cols] = ka.astype(bf16)
            kv_ref[rows, 2 * D_KVH + j * HEAD_DIM:2 * D_KVH + (j + 1) * HEAD_DIM] = (
                kb.astype(bf16))
            if not rope:
                dst = pl.ds(j, SEQ, stride=N_KV_A)
                ca_ref[s, 0, dst, :] = ka
                ca_ref[s, 1, dst, :] = va[:, cols]
                cb_ref[s, 0, dst, :] = kb
                cb_ref[s, 1, dst, :] = vb[:, cols]

    def emit(rope):
        hs = []
        for s in range(TM // SEQ):
            x = x_refs[-1 if rope else 0][s * SEQ:(s + 1) * SEQ, :]
            hs.append((_ln(x) * (1.0 + scale_ref[...]) + shift_ref[...]).astype(bf16))
        qkvs = [jnp.dot(h, w_ref[...], preferred_element_type=f32) for h in hs]
        for s, qkv in enumerate(qkvs):
            emit_rows(qkv, s, rope)

    @pl.when(is_ctx)
    def _():
        emit(False)
        if n_alias == 0:
            rest = ca_ref.shape[1] - 2
            for c_ref in (ca_ref, cb_ref):
                c_ref[:, 2:] = jnp.zeros((TM // SEQ, rest, SEQ * N_KV_A, HEAD_DIM), f32)

    @pl.when(jnp.logical_not(is_ctx))
    def _():
        emit(True)


def _qkv(xs, mod, w_in, qg, kg, rope, caches, layer):
    rope_spec = pl.BlockSpec(
        (TM, HEAD_DIM), lambda i: (jnp.maximum(i - N_CTX_TILES, 0) % TILES_PER_LAT_SEQ, 0))
    assert caches or layer == 0
    slots, slot0 = (2 * DEPTH, 0) if not caches else (2, layer)
    cache_spec = pl.BlockSpec((TM // SEQ, slots, SEQ * N_KV_A, HEAD_DIM),
                              lambda i: (jnp.minimum(i, N_CTX_TILES - 1), slot0, 0, 0))
    cache_shape = jax.ShapeDtypeStruct((BATCH, DEPTH * 2, SEQ * N_KV_A, HEAD_DIM), f32)
    n_in = len(xs) + 8
    return pl.pallas_call(
        functools.partial(_qkv_kernel, n_x=len(xs), n_alias=len(caches)),
        grid=(N_TILES,),
        in_specs=_row_specs(len(xs) == 2) + [
            _mod_spec(layer, 0), _mod_spec(layer, 1),
            pl.BlockSpec((None, D_MODEL, D_IN), lambda i: (layer, 0, 0),
                         pipeline_mode=pl.Buffered(1)),
            _layer_vec_spec(layer, HEAD_DIM), _layer_vec_spec(layer, HEAD_DIM),
            rope_spec, rope_spec, rope_spec,
        ] + [pl.BlockSpec(memory_space=pl.ANY)] * len(caches),
        out_specs=[
            pl.BlockSpec((TM, D_Q), lambda i: (i, 0)),
            pl.BlockSpec((TM, D_KV), lambda i: (i, 0)),
            cache_spec, cache_spec,
        ],
        out_shape=[
            jax.ShapeDtypeStruct((M_ALL, D_Q), bf16),
            jax.ShapeDtypeStruct((M_ALL, D_KV), bf16),
            cache_shape, cache_shape,
        ],
        input_output_aliases={n_in + k: 2 + k for k in range(len(caches))},
        compiler_params=_params("arbitrary"),
    )(*xs, mod, mod, w_in, qg, kg, *rope, *caches)


def _stack_heads(q_ref, first_head):
    return jnp.concatenate(
        [q_ref[:, (first_head + j) * HEAD_DIM:(first_head + j + 1) * HEAD_DIM]
         for j in range(GROUP)], axis=0)


def _sink_column(sink_ref, first, rows):
    return jnp.concatenate(
        [jnp.full((rows, 1), sink_ref[first + j] * LOG2E, f32) for j in range(GROUP)], axis=0)


def _logits(qs, k):
    return lax.dot_general(qs, k, (((1,), (1,)), ((), ())), preferred_element_type=f32)


def _with_ones(v):
    return jnp.concatenate([v, jnp.ones_like(v)], axis=1)


def _attend_all(jobs):
    ss = []
    for qs, k, _, bias, _ in jobs:
        s = _logits(qs, k)
        ss.append(s if bias is None else s + bias)
    ms = []
    for s, (_, _, _, _, sink) in zip(ss, jobs):
        m = jnp.max(s, axis=-1, keepdims=True)
        ms.append(m if sink is None else jnp.maximum(m, sink))
    ps = [jnp.exp2(s - m).astype(bf16) for s, m in zip(ss, ms)]
    outs = []
    for p, m, (_, _, v1, _, sink) in zip(ps, ms, jobs):
        o = jnp.dot(p, v1, preferred_element_type=f32)
        l = o[:, HEAD_DIM:]
        if sink is not None:
            l = l + jnp.exp2(sink - m)
        outs.append(o[:, :HEAD_DIM] * (1.0 / l))
    return outs


def _store_heads(o_ref, o, first_head, rows):
    for j in range(GROUP):
        c0 = (first_head + j) * HEAD_DIM
        o_ref[:, c0:c0 + HEAD_DIM] = o[j * rows:(j + 1) * rows].astype(o_ref.dtype)


def _ctx_attn_kernel(sink_ref, q_ref, kv_ref, o_ref, *, layer):
    for mixer in range(2):
        for g in range(2):
            kcol = mixer * 2 * D_KVH + g * HEAD_DIM
            k = kv_ref[:, kcol:kcol + HEAD_DIM]
            v = kv_ref[:, kcol + D_KVH:kcol + D_KVH + HEAD_DIM]
            first = mixer * N_HEADS_A + g * GROUP
            sink = None
            if mixer == 0:
                sink = _sink_column(sink_ref, layer * N_HEADS_A + g * GROUP, SEQ)
            o, = _attend_all([(_stack_heads(q_ref, first), k, _with_ones(v), None, sink)])
            _store_heads(o_ref, o, first, SEQ)


def _ctx_attn(sink, q, kv, layer):
    return pl.pallas_call(
        functools.partial(_ctx_attn_kernel, layer=layer),
        grid=(BATCH,),
        in_specs=[
            pl.BlockSpec(memory_space=pltpu.SMEM),
            pl.BlockSpec((SEQ, D_Q), lambda b: (b, 0)),
            pl.BlockSpec((SEQ, D_KV), lambda b: (b, 0)),
        ],
        out_specs=pl.BlockSpec((SEQ, D_Q), lambda b: (b, 0)),
        out_shape=jax.ShapeDtypeStruct((M_CTX, D_Q), bf16),
        compiler_params=_params("arbitrary"),
    )(sink, q, kv)


N_QB = DEC_SEQ // BLOCK
S_B = PAST_LEN + DEC_SEQ
A_PAD = BLOCK
A_ROWS = PAST_LEN + A_PAD + DEC_SEQ + A_PAD


def _window_bias():
    a = np.arange(BLOCK)[:, None]
    j = np.arange(3 * BLOCK)[None, :]
    near = (j >= a) & (j <= a + 2 * BLOCK)
    out = []
    for lo, hi in ((BLOCK, 3 * BLOCK), (0, 3 * BLOCK), (0, 2 * BLOCK)):
        ok = near & (j >= lo) & (j < hi)
        out.append(np.concatenate(
            [np.zeros((BLOCK, PAST_LEN), np.float32), np.where(ok, 0.0, NEG).astype(np.float32)],
            axis=1))
    return jnp.asarray(np.stack(out))


def _lat_attn_kernel(sink_ref, q_ref, kva_ref, kvb_ref, ca_ref, cb_ref, bias_ref,
                     o_ref, ka_s, va_s, kb_s, vb_s, *, layer):
    n = pl.program_id(1)

    @pl.when(n == 0)
    def _():
        lat_a = slice(PAST_LEN + A_PAD, PAST_LEN + A_PAD + DEC_SEQ)
        ka_s[0:PAST_LEN, :] = ca_ref[0].astype(bf16)
        ka_s[lat_a, :] = kva_ref[:, 0:D_KVH]
        kb_s[0:PAST_LEN, :] = cb_ref[0].astype(bf16)
        kb_s[PAST_LEN:, :] = kvb_ref[:, 0:D_KVH]
        for pad0 in (PAST_LEN, PAST_LEN + A_PAD + DEC_SEQ):
            ka_s[pad0:pad0 + A_PAD, :] = jnp.zeros((A_PAD, D_KVH), bf16)
            va_s[pad0:pad0 + A_PAD, :] = jnp.zeros((A_PAD, 2 * D_KVH), bf16)
        for g in range(N_KV_A):
            src = slice(g * HEAD_DIM, (g + 1) * HEAD_DIM)
            vsrc = slice(D_KVH + g * HEAD_DIM, D_KVH + (g + 1) * HEAD_DIM)
            dst = slice(2 * g * HEAD_DIM, (2 * g + 1) * HEAD_DIM)
            one = slice((2 * g + 1) * HEAD_DIM, (2 * g + 2) * HEAD_DIM)
            va_s[0:PAST_LEN, dst] = ca_ref[1, :, src].astype(bf16)
            va_s[lat_a, dst] = kva_ref[:, vsrc]
            va_s[0:PAST_LEN, one] = jnp.ones((PAST_LEN, HEAD_DIM), bf16)
            va_s[lat_a, one] = jnp.ones((DEC_SEQ, HEAD_DIM), bf16)
            vb_s[0:PAST_LEN, dst] = cb_ref[1, :, src].astype(bf16)
            vb_s[PAST_LEN:, dst] = kvb_ref[:, vsrc]
            vb_s[:, one] = jnp.ones((S_B, HEAD_DIM), bf16)

    win = pl.ds(pl.multiple_of(PAST_LEN + n * BLOCK, BLOCK), 3 * BLOCK)
    bias = jnp.concatenate([bias_ref[...]] * GROUP, axis=0)
    jobs, firsts = [], []
    for g in range(N_KV_A):
        kc = slice(g * HEAD_DIM, (g + 1) * HEAD_DIM)
        vc = slice(2 * g * HEAD_DIM, (2 * g + 2) * HEAD_DIM)
        k = jnp.concatenate([ka_s[0:PAST_LEN, kc], ka_s[win, kc]], axis=0)
        v1 = jnp.concatenate([va_s[0:PAST_LEN, vc], va_s[win, vc]], axis=0)
        sink = _sink_column(sink_ref, layer * N_HEADS_A + g * GROUP, BLOCK)
        jobs.append((_stack_heads(q_ref, g * GROUP), k, v1, bias, sink))
        firsts.append(g * GROUP)

    for g in range(N_KV_B):
        kc = slice(g * HEAD_DIM, (g + 1) * HEAD_DIM)
        first = N_HEADS_A + g * GROUP
        jobs.append((_stack_heads(q_ref, first), kb_s[:, kc],
                     vb_s[:, 2 * g * HEAD_DIM:(2 * g + 2) * HEAD_DIM], None, None))
        firsts.append(first)
    for first, o in zip(firsts, _attend_all(jobs)):
        _store_heads(o_ref, o, first, BLOCK)


def _lat_attn(sink, q, kv, cache_a, cache_b, bias, layer):
    blk0 = M_CTX // BLOCK
    cache_spec = pl.BlockSpec((None, None, 2, PAST_LEN, D_KVH), lambda b, n: (b, layer, 0, 0, 0))
    half = 2 * D_KVH
    seq0 = M_CTX // DEC_SEQ
    return pl.pallas_call(
        functools.partial(_lat_attn_kernel, layer=layer),
        grid=(DEC_BATCH, N_QB),
        in_specs=[
            pl.BlockSpec(memory_space=pltpu.SMEM),
            pl.BlockSpec((BLOCK, D_Q), lambda b, n: (blk0 + b * N_QB + n, 0)),
            pl.BlockSpec((DEC_SEQ, half), lambda b, n: (seq0 + b, 0)),
            pl.BlockSpec((DEC_SEQ, half), lambda b, n: (seq0 + b, 1)),
            cache_spec, cache_spec,
            pl.BlockSpec((None, BLOCK, PAST_LEN + 3 * BLOCK),
                         lambda b, n: (jnp.where(n == 0, 0, jnp.where(n == N_QB - 1, 2, 1)), 0, 0)),
        ],
        out_specs=pl.BlockSpec((BLOCK, D_Q), lambda b, n: (b * N_QB + n, 0)),
        out_shape=jax.ShapeDtypeStruct((M_LAT, D_Q), bf16),
        scratch_shapes=[
            pltpu.VMEM((A_ROWS, D_KVH), bf16), pltpu.VMEM((A_ROWS, 2 * D_KVH), bf16),
            pltpu.VMEM((S_B, D_KVH), bf16), pltpu.VMEM((S_B, 2 * D_KVH), bf16)],
        compiler_params=_params("arbitrary", "arbitrary"),
    )(sink, q, kv, kv, cache_a, cache_b, bias)


def _oproj_kernel(*refs, n_x):
    oc_ref, ol_ref = refs[:2]
    x_refs = refs[2:2 + n_x]
    w_ref, gate_ref, g_ref, b_ref, shift_ref, scale_ref, y_ref, h_ref = refs[2 + n_x:]
    is_ctx = pl.program_id(0) < N_CTX_TILES
    o = jnp.where(is_ctx, oc_ref[...], ol_ref[...])
    x = _row_load(x_refs, is_ctx)
    subs = [slice(r0, r0 + SEQ) for r0 in range(0, TM, SEQ)]
    fs = [jnp.dot(o[rows], w_ref[...], preferred_element_type=f32) for rows in subs]
    for rows, f in zip(subs, fs):
        y = _ln(ALPHA * x[rows] + gate_ref[...] * f) * g_ref[...] + b_ref[...]
        y_ref[rows, :] = y
        h_ref[rows, :] = (_ln(y) * (1.0 + scale_ref[...]) + shift_ref[...]).astype(bf16)


def _oproj(o_ctx, o_lat, xs, w_o, mod, ln_g, ln_b, layer):
    return pl.pallas_call(
        functools.partial(_oproj_kernel, n_x=len(xs)),
        grid=(N_TILES,),
        in_specs=[
            pl.BlockSpec((TM, D_Q), lambda i: (jnp.minimum(i, N_CTX_TILES - 1), 0)),
            pl.BlockSpec((TM, D_Q), lambda i: (jnp.maximum(i - N_CTX_TILES, 0), 0)),
        ] + _row_specs(len(xs) == 2) + [
            pl.BlockSpec((None, D_Q, D_MODEL), lambda i: (layer, 0, 0),
                         pipeline_mode=pl.Buffered(1)),
            _mod_spec(layer, 2), _layer_vec_spec(layer, D_MODEL), _layer_vec_spec(layer, D_MODEL),
            _mod_spec(layer, 3), _mod_spec(layer, 4),
        ],
        out_specs=[
            pl.BlockSpec((TM, D_MODEL), lambda i: (i, 0)),
            pl.BlockSpec((TM, D_MODEL), lambda i: (i, 0)),
        ],
        out_shape=[
            jax.ShapeDtypeStruct((M_ALL, D_MODEL), f32),
            jax.ShapeDtypeStruct((M_ALL, D_MODEL), bf16),
        ],
        compiler_params=_params("arbitrary"),
    )(o_ctx, o_lat, *xs, w_o, mod, ln_g, ln_b, mod, mod)


STEP_ROWS = 2 * TM
STEPS_PER_LAT_SEQ = DEC_SEQ // STEP_ROWS


def _conv_rows(u, cw, cb, is_ctx):
    mid_rows = slice(HALO, HALO + STEP_ROWS)
    um = pltpu.roll(u, 1, 0)[mid_rows]
    up = pltpu.roll(u, u.shape[0] - 1, 0)[mid_rows]
    lo, mid, hi = um * cw[0:1], u[mid_rows] * cw[1:2] + cb, up * cw[2:3]
    out = lo + mid + hi
    pieces, done = [], 0
    for b in range(SEQ, STEP_ROWS, SEQ):
        rows = slice(b - 8, b + 8)
        r = lax.broadcasted_iota(jnp.int32, (16, u.shape[1]), 0) + (b - 8)
        fixed = jnp.where(r == b - 1, lo[rows] + mid[rows],
                          jnp.where(r == b, mid[rows] + hi[rows], out[rows]))
        pieces += [out[done:b - 8], jnp.where(is_ctx, fixed, out[rows])]
        done = b + 8
    return jnp.concatenate(pieces + [out[done:]], axis=0)


def _ffn_up_kernel(hp_ref, h_ref, hn_ref, wv_ref, wg_ref, cwv_ref, cwg_ref, cbv_ref, cbg_ref,
                   a_ref, w_s, hcat):
    j = pl.program_id(1)

    @pl.when(j == 0)
    def _():
        w_s[:, :TF] = wv_ref[...].astype(bf16)
        w_s[:, TF:] = wg_ref[...].astype(bf16)

    n_ctx = M_CTX // STEP_ROWS
    is_ctx = j < n_ctx
    t = (j - n_ctx) % STEPS_PER_LAT_SEQ
    zero = jnp.zeros((HALO, D_MODEL), bf16)
    hcat[0:HALO, :] = jnp.where(is_ctx | (t == 0), zero, hp_ref[...])
    hcat[HALO:HALO + STEP_ROWS, :] = h_ref[...]
    hcat[HALO + STEP_ROWS:, :] = jnp.where(is_ctx | (t == STEPS_PER_LAT_SEQ - 1), zero, hn_ref[...])
    u = jnp.dot(hcat[...], w_s[...], preferred_element_type=f32)
    val = _conv_rows(u[:, :TF], cwv_ref[...], cbv_ref[...], is_ctx)
    gate = _conv_rows(u[:, TF:], cwg_ref[...], cbg_ref[...], is_ctx)
    a_ref[...] = (gate * jax.nn.sigmoid(gate) * val).astype(bf16)


def _ffn_up(h, w_up, conv_w, conv_b, layer):
    hb = STEP_ROWS // HALO
    return pl.pallas_call(
        _ffn_up_kernel,
        grid=(N_F, M_ALL // STEP_ROWS),
        in_specs=[
            pl.BlockSpec((HALO, D_MODEL), lambda f, j: (jnp.maximum(j * hb - 1, 0), 0)),
            pl.BlockSpec((STEP_ROWS, D_MODEL), lambda f, j: (j, 0)),
            pl.BlockSpec((HALO, D_MODEL),
                         lambda f, j: (jnp.minimum((j + 1) * hb, M_ALL // HALO - 1), 0)),
            pl.BlockSpec((None, D_MODEL, TF), lambda f, j: (layer, 0, f)),
            pl.BlockSpec((None, D_MODEL, TF), lambda f, j: (layer, 0, N_F + f)),
            pl.BlockSpec((None, 3, TF), lambda f, j: (layer, 0, f)),
            pl.BlockSpec((None, 3, TF), lambda f, j: (layer, 0, N_F + f)),
            pl.BlockSpec((None, 1, TF), lambda f, j: (layer, 0, f)),
            pl.BlockSpec((None, 1, TF), lambda f, j: (layer, 0, N_F + f)),
        ],
        out_specs=pl.BlockSpec((STEP_ROWS, TF), lambda f, j: (j, f)),
        out_shape=jax.ShapeDtypeStruct((M_ALL, D_FF), bf16),
        scratch_shapes=[pltpu.VMEM((D_MODEL, 2 * TF), bf16),
                        pltpu.VMEM((STEP_ROWS + 2 * HALO, D_MODEL), bf16)],
        compiler_params=_params("arbitrary", "arbitrary"),
    )(h, h, h, w_up, w_up, conv_w, conv_w, conv_b, conv_b)


def _ffn_down_kernel(a_ref, y_ref, w_ref, gate_ref, g_ref, b_ref, *o_refs):
    f = jnp.dot(a_ref[...], w_ref[...], preferred_element_type=f32)
    out = _ln(ALPHA * y_ref[...] + gate_ref[...] * f) * g_ref[...] + b_ref[...]
    if len(o_refs) == 1:
        o_refs[0][...] = out
    else:
        is_ctx = pl.program_id(0) < M_CTX // TM_DOWN

        @pl.when(is_ctx)
        def _():
            o_refs[0][...] = out

        @pl.when(jnp.logical_not(is_ctx))
        def _():
            o_refs[1][...] = out


def _ffn_down(a, y, w_down, mod, ln_g, ln_b, layer, split_out):
    if split_out:
        out_shape = [jax.ShapeDtypeStruct((M_CTX, D_MODEL), f32),
                     jax.ShapeDtypeStruct((M_LAT, D_MODEL), f32)]
    else:
        out_shape = [jax.ShapeDtypeStruct((M_ALL, D_MODEL), f32)]
    return pl.pallas_call(
        _ffn_down_kernel,
        grid=(M_ALL // TM_DOWN,),
        in_specs=[
            pl.BlockSpec((TM_DOWN, D_FF), lambda i: (i, 0)),
            pl.BlockSpec((TM_DOWN, D_MODEL), lambda i: (i, 0)),
            pl.BlockSpec((None, D_FF, D_MODEL), lambda i: (layer, 0, 0),
                         pipeline_mode=pl.Buffered(1)),
            _mod_spec(layer, 5, tm=TM_DOWN),
            _layer_vec_spec(layer, D_MODEL), _layer_vec_spec(layer, D_MODEL),
        ],
        out_specs=_row_specs(split_out, tm=TM_DOWN),
        out_shape=out_shape,
        compiler_params=_params("arbitrary"),
    )(a, y, w_down, mod, ln_g, ln_b)


def _rope_tables():
    pos = np.arange(DEC_SEQ)
    q4 = HEAD_DIM // 4
    freq = jnp.asarray(ROPE_THETA, f32) ** (-jnp.arange(q4, dtype=f32) / q4)
    row = jnp.asarray(pos // GRID_W, f32)[:, None] * freq
    col = jnp.asarray(pos % GRID_W, f32)[:, None] * freq
    zero = jnp.zeros_like(row)
    c = jnp.concatenate([jnp.cos(row), jnp.cos(row), jnp.cos(col), jnp.cos(col)], axis=1)
    a = jnp.concatenate([-jnp.sin(row), zero, -jnp.sin(col), zero], axis=1)
    b = jnp.concatenate([zero, jnp.sin(row), zero, jnp.sin(col)], axis=1)
    return c, a, b


def kernel(x_prompt, x_sample, cache_attn_a, cache_attn_b, c, c_ctx, w_ada, b_ada, w_in,
           q_norm_g, k_norm_g, sink_a, w_o, ln1_g, ln1_b, w_up, conv_w, conv_b, w_down,
           ln2_g, ln2_b):
    cvecs = jnp.concatenate(
        [c_ctx[None], c, jnp.zeros((N_MOD_ROWS - 1 - DEC_BATCH, D_MODEL), f32)], axis=0)
    mod = _adaln(cvecs, w_ada, b_ada).reshape(DEPTH * N_MOD_ROWS * 6, 1, D_MODEL)

    w_in_b = w_in.astype(bf16)
    w_o_b = w_o.astype(bf16)
    w_down_b = w_down.astype(bf16)
    rope = _rope_tables()
    win_bias = _window_bias()
    sink = sink_a.reshape(DEPTH * N_HEADS_A)
    cache_a = cache_attn_a.reshape(DEC_BATCH, DEPTH, 2, PAST_LEN, D_KVH)
    cache_b = cache_attn_b.reshape(DEC_BATCH, DEPTH, 2, PAST_LEN, D_KVH)
    per_layer = lambda v: v.reshape(DEPTH, 1, v.shape[-1])
    qg, kg = per_layer(q_norm_g), per_layer(k_norm_g)
    g1, b1, g2, b2 = per_layer(ln1_g), per_layer(ln1_b), per_layer(ln2_g), per_layer(ln2_b)
    conv_b3 = per_layer(conv_b)

    xs = [x_prompt.reshape(M_CTX, D_MODEL), x_sample.reshape(M_LAT, D_MODEL)]
    new_caches = []
    for l in range(DEPTH):
        q, kv, new_a, new_b = _qkv(xs, mod, w_in_b, qg, kg, rope, new_caches, l)
        new_caches = [new_a, new_b]
        o_ctx = _ctx_attn(sink, q, kv, l)
        o_lat = _lat_attn(sink, q, kv, cache_a, cache_b, win_bias, l)
        y, h = _oproj(o_ctx, o_lat, xs, w_o_b, mod, g1, b1, l)
        act = _ffn_up(h, w_up, conv_w, conv_b3, l)
        xs = _ffn_down(act, y, w_down_b, mod, g2, b2, l, split_out=(l == DEPTH - 1))

    cache_shape = (BATCH, DEPTH, 2, SEQ, N_KV_A, HEAD_DIM)
    return (xs[0].reshape(BATCH, SEQ, D_MODEL), xs[1].reshape(DEC_BATCH, DEC_SEQ, D_MODEL),
            new_caches[0].reshape(cache_shape), new_caches[1].reshape(cache_shape))
```

```python
import functools

import jax
import jax.numpy as jnp
import numpy as np
from jax import lax
from jax.experimental import pallas as pl
from jax.experimental.pallas import tpu as pltpu

D_MODEL = 2048
BATCH = 32
SEQ = 256
DEPTH = 2
DEC_BATCH = 2
DEC_SEQ = 2048
PAST_LEN = 256
GRID_W = 64
HEAD_DIM = 128
N_HEADS_A = 8
N_KV_A = 2
N_HEADS_B = 8
N_KV_B = 2
GROUP = 4
BLOCK = 128
D_FF = 5632
ROPE_THETA = 10000.0
LN_EPS = 1e-6
ALPHA = (2.0 * DEPTH) ** 0.25
SCALE = HEAD_DIM ** -0.5
LOG2E = 1.4426950408889634
QSCALE = SCALE * LOG2E
NEG = -1e30

D_Q = (N_HEADS_A + N_HEADS_B) * HEAD_DIM
D_KVH = N_KV_A * HEAD_DIM
D_KV = 4 * D_KVH
D_IN = D_Q + D_KV
OFF_QA, OFF_KA, OFF_VA = 0, 1024, 1280
OFF_QB, OFF_KB, OFF_VB = 1536, 2560, 2816

M_CTX = BATCH * SEQ
M_LAT = DEC_BATCH * DEC_SEQ
M_ALL = M_CTX + M_LAT
N_MOD_ROWS = 8

TM = 2 * SEQ
N_TILES = M_ALL // TM
N_CTX_TILES = M_CTX // TM
TILES_PER_LAT_SEQ = DEC_SEQ // TM
TM_DOWN = 256
HALO = 16
TF = 512
N_F = D_FF // TF
TN_ADA = 1024
VMEM_LIMIT = 60 * 1024 * 1024

f32 = jnp.float32
bf16 = jnp.bfloat16


def _params(*sem, flags=None):
    return pltpu.CompilerParams(
        dimension_semantics=sem, vmem_limit_bytes=VMEM_LIMIT, flags=flags)


def _ln(x):
    mu = jnp.mean(x, axis=-1, keepdims=True)
    xc = x - mu
    var = jnp.mean(xc * xc, axis=-1, keepdims=True)
    return xc * lax.rsqrt(var + LN_EPS)


def _mod_spec(layer, which, tm=TM):
    n_ctx = M_CTX // tm
    per_seq = DEC_SEQ // tm

    def index_map(i):
        row = jnp.where(i < n_ctx, 0, 1 + (i - n_ctx) // per_seq)
        return ((layer * N_MOD_ROWS + row) * 6 + which, 0, 0)
    return pl.BlockSpec((None, 1, D_MODEL), index_map)


def _layer_vec_spec(layer, width):
    return pl.BlockSpec((None, 1, width), lambda i: (layer, 0, 0))


def _row_specs(split, tm=TM):
    n_ctx = M_CTX // tm
    if not split:
        return [pl.BlockSpec((tm, D_MODEL), lambda i: (i, 0))]
    return [pl.BlockSpec((tm, D_MODEL), lambda i: (jnp.minimum(i, n_ctx - 1), 0)),
            pl.BlockSpec((tm, D_MODEL), lambda i: (jnp.maximum(i - n_ctx, 0), 0))]


def _row_load(refs, is_ctx):
    if len(refs) == 1:
        return refs[0][...]
    return jnp.where(is_ctx, refs[0][...], refs[1][...])


def _adaln_kernel(cv_ref, w_ref, b_ref, o_ref):
    cv = cv_ref[...]
    a = (cv * jax.nn.sigmoid(cv)).astype(bf16)
    o_ref[...] = jnp.dot(a, w_ref[...].astype(bf16), preferred_element_type=f32) + b_ref[...]


def _adaln(cvecs, w_ada, b_ada):
    return pl.pallas_call(
        _adaln_kernel,
        grid=(DEPTH, 6 * D_MODEL // TN_ADA),
        in_specs=[
            pl.BlockSpec((N_MOD_ROWS, D_MODEL), lambda l, n: (0, 0)),
            pl.BlockSpec((None, D_MODEL, TN_ADA), lambda l, n: (l, 0, n)),
            pl.BlockSpec((None, 1, TN_ADA), lambda l, n: (l, 0, n)),
        ],
        out_specs=pl.BlockSpec((None, N_MOD_ROWS, TN_ADA), lambda l, n: (l, 0, n)),
        out_shape=jax.ShapeDtypeStruct((DEPTH, N_MOD_ROWS, 6 * D_MODEL), f32),
        compiler_params=_params("arbitrary", "arbitrary"),
    )(cvecs, w_ada, b_ada.reshape(DEPTH, 1, 6 * D_MODEL))


def _qkv_kernel(*refs, n_x, n_alias):
    x_refs = refs[:n_x]
    (shift_ref, scale_ref, w_ref, qg_ref, kg_ref, rc_ref, ra_ref, rb_ref) = refs[n_x:n_x + 8]
    q_ref, kv_ref, ca_ref, cb_ref = refs[n_x + 8 + n_alias:]
    i = pl.program_id(0)
    is_ctx = i < N_CTX_TILES

    def head(qkv, rows, col, gain, rope):
        xh = qkv[:, col:col + HEAD_DIM]
        if gain is not None:
            ms = jnp.mean(xh * xh, axis=-1, keepdims=True)
            xh = xh * lax.rsqrt(ms + LN_EPS) * gain
        if rope:
            xh = (xh * rc_ref[rows, :] + pltpu.roll(xh, HEAD_DIM - 32, 1) * ra_ref[rows, :]
                  + pltpu.roll(xh, 32, 1) * rb_ref[rows, :])
        return xh

    def emit_rows(qkv, s, rope):
        rows = slice(s * SEQ, (s + 1) * SEQ)
        qg = qg_ref[...]
        kg = kg_ref[...]
        for hh in range(N_HEADS_A):
            q_ref[rows, hh * HEAD_DIM:(hh + 1) * HEAD_DIM] = (
                head(qkv, rows, OFF_QA + hh * HEAD_DIM, None, rope) * QSCALE).astype(bf16)
        for hh in range(N_HEADS_B):
            c0 = (N_HEADS_A + hh) * HEAD_DIM
            q_ref[rows, c0:c0 + HEAD_DIM] = (
                head(qkv, rows, OFF_QB + hh * HEAD_DIM, qg, rope) * QSCALE).astype(bf16)
        va = qkv[:, OFF_VA:OFF_VA + D_KVH]
        vb = qkv[:, OFF_VB:OFF_VB + D_KVH]
        kv_ref[rows, D_KVH:2 * D_KVH] = va.astype(bf16)
        kv_ref[rows, 3 * D_KVH:] = vb.astype(bf16)
        for j in range(N_KV_A):
            cols = slice(j * HEAD_DIM, (j + 1) * HEAD_DIM)
            ka = head(qkv, rows, OFF_KA + j * HEAD_DIM, None, rope)
            kb = head(qkv, rows, OFF_KB + j * HEAD_DIM, kg, rope)
            kv_ref[rows, cols] = ka.astype(bf16)
            kv_ref[rows, 2 * D_KVH + j * HEAD_DIM:2 * D_KVH + (j + 1) * HEAD_DIM] = (
                kb.astype(bf16))
            if not rope:
                dst = pl.ds(j, SEQ, stride=N_KV_A)
                ca_ref[s, 0, dst, :] = ka
                ca_ref[s, 1, dst, :] = va[:, cols]
                cb_ref[s, 0, dst, :] = kb
                cb_ref[s, 1, dst, :] = vb[:, cols]

    def emit(rope):
        hs = []
        for s in range(TM // SEQ):
            x = x_refs[-1 if rope else 0][s * SEQ:(s + 1) * SEQ, :]
            hs.append((_ln(x) * (1.0 + scale_ref[...]) + shift_ref[...]).astype(bf16))
        qkvs = [jnp.dot(h, w_ref[...], preferred_element_type=f32) for h in hs]
        for s, qkv in enumerate(qkvs):
            emit_rows(qkv, s, rope)

    @pl.when(is_ctx)
    def _():
        emit(False)
        if n_alias == 0:
            rest = ca_ref.shape[1] - 2
            for c_ref in (ca_ref, cb_ref):
                c_ref[:, 2:] = jnp.zeros((TM // SEQ, rest, SEQ * N_KV_A, HEAD_DIM), f32)

    @pl.when(jnp.logical_not(is_ctx))
    def _():
        emit(True)


def _qkv(xs, mod, w_in, qg, kg, rope, caches, layer):
    rope_spec = pl.BlockSpec(
        (TM, HEAD_DIM), lambda i: (jnp.maximum(i - N_CTX_TILES, 0) % TILES_PER_LAT_SEQ, 0))
    assert caches or layer == 0
    slots, slot0 = (2 * DEPTH, 0) if not caches else (2, layer)
    cache_spec = pl.BlockSpec((TM // SEQ, slots, SEQ * N_KV_A, HEAD_DIM),
                              lambda i: (jnp.minimum(i, N_CTX_TILES - 1), slot0, 0, 0))
    cache_shape = jax.ShapeDtypeStruct((BATCH, DEPTH * 2, SEQ * N_KV_A, HEAD_DIM), f32)
    n_in = len(xs) + 8
    return pl.pallas_call(
        functools.partial(_qkv_kernel, n_x=len(xs), n_alias=len(caches)),
        grid=(N_TILES,),
        in_specs=_row_specs(len(xs) == 2) + [
            _mod_spec(layer, 0), _mod_spec(layer, 1),
            pl.BlockSpec((None, D_MODEL, D_IN), lambda i: (layer, 0, 0),
                         pipeline_mode=pl.Buffered(1)),
            _layer_vec_spec(layer, HEAD_DIM), _layer_vec_spec(layer, HEAD_DIM),
            rope_spec, rope_spec, rope_spec,
        ] + [pl.BlockSpec(memory_space=pl.ANY)] * len(caches),
        out_specs=[
            pl.BlockSpec((TM, D_Q), lambda i: (i, 0)),
            pl.BlockSpec((TM, D_KV), lambda i: (i, 0)),
            cache_spec, cache_spec,
        ],
        out_shape=[
            jax.ShapeDtypeStruct((M_ALL, D_Q), bf16),
            jax.ShapeDtypeStruct((M_ALL, D_KV), bf16),
            cache_shape, cache_shape,
        ],
        input_output_aliases={n_in + k: 2 + k for k in range(len(caches))},
        compiler_params=_params("arbitrary"),
    )(*xs, mod, mod, w_in, qg, kg, *rope, *caches)


def _stack_heads(q_ref, first_head):
    return jnp.concatenate(
        [q_ref[:, (first_head + j) * HEAD_DIM:(first_head + j + 1) * HEAD_DIM]
         for j in range(GROUP)], axis=0)


def _sink_column(sink_ref, first, rows):
    return jnp.concatenate(
        [jnp.full((rows, 1), sink_ref[first + j] * LOG2E, f32) for j in range(GROUP)], axis=0)


def _logits(qs, k):
    return lax.dot_general(qs, k, (((1,), (1,)), ((), ())), preferred_element_type=f32)


def _with_ones(v):
    return jnp.concatenate([v, jnp.ones_like(v)], axis=1)


def _attend_all(jobs):
    ss = []
    for qs, k, _, bias, _ in jobs:
        s = _logits(qs, k)
        ss.append(s if bias is None else s + bias)
    ms = []
    for s, (_, _, _, _, sink) in zip(ss, jobs):
        m = jnp.max(s, axis=-1, keepdims=True)
        ms.append(m if sink is None else jnp.maximum(m, sink))
    ps = [jnp.exp2(s - m).astype(bf16) for s, m in zip(ss, ms)]
    outs = []
    for p, m, (_, _, v1, _, sink) in zip(ps, ms, jobs):
        o = jnp.dot(p, v1, preferred_element_type=f32)
        l = o[:, HEAD_DIM:]
        if sink is not None:
            l = l + jnp.exp2(sink - m)
        outs.append(o[:, :HEAD_DIM] * (1.0 / l))
    return outs


def _store_heads(o_ref, o, first_head, rows):
    for j in range(GROUP):
        c0 = (first_head + j) * HEAD_DIM
        o_ref[:, c0:c0 + HEAD_DIM] = o[j * rows:(j + 1) * rows].astype(o_ref.dtype)


def _ctx_attn_kernel(sink_ref, q_ref, kv_ref, o_ref, *, layer):
    for mixer in range(2):
        for g in range(2):
            kcol = mixer * 2 * D_KVH + g * HEAD_DIM
            k = kv_ref[:, kcol:kcol + HEAD_DIM]
            v = kv_ref[:, kcol + D_KVH:kcol + D_KVH + HEAD_DIM]
            first = mixer * N_HEADS_A + g * GROUP
            sink = None
            if mixer == 0:
                sink = _sink_column(sink_ref, layer * N_HEADS_A + g * GROUP, SEQ)
            o, = _attend_all([(_stack_heads(q_ref, first), k, _with_ones(v), None, sink)])
            _store_heads(o_ref, o, first, SEQ)


def _ctx_attn(sink, q, kv, layer):
    return pl.pallas_call(
        functools.partial(_ctx_attn_kernel, layer=layer),
        grid=(BATCH,),
        in_specs=[
            pl.BlockSpec(memory_space=pltpu.SMEM),
            pl.BlockSpec((SEQ, D_Q), lambda b: (b, 0)),
            pl.BlockSpec((SEQ, D_KV), lambda b: (b, 0)),
        ],
        out_specs=pl.BlockSpec((SEQ, D_Q), lambda b: (b, 0)),
        out_shape=jax.ShapeDtypeStruct((M_CTX, D_Q), bf16),
        compiler_params=_params("arbitrary"),
    )(sink, q, kv)


N_QB = DEC_SEQ // BLOCK
S_B = PAST_LEN + DEC_SEQ
A_PAD = BLOCK
A_ROWS = PAST_LEN + A_PAD + DEC_SEQ + A_PAD


def _window_bias():
    a = np.arange(BLOCK)[:, None]
    j = np.arange(3 * BLOCK)[None, :]
    near = (j >= a) & (j <= a + 2 * BLOCK)
    out = []
    for lo, hi in ((BLOCK, 3 * BLOCK), (0, 3 * BLOCK), (0, 2 * BLOCK)):
        ok = near & (j >= lo) & (j < hi)
        out.append(np.concatenate(
            [np.zeros((BLOCK, PAST_LEN), np.float32), np.where(ok, 0.0, NEG).astype(np.float32)],
            axis=1))
    return jnp.asarray(np.stack(out))


def _lat_attn_kernel(sink_ref, q_ref, kva_ref, kvb_ref, ca_ref, cb_ref, bias_ref,
                     o_ref, ka_s, va_s, kb_s, vb_s, *, layer):
    n = pl.program_id(1)

    @pl.when(n == 0)
    def _():
        lat_a = slice(PAST_LEN + A_PAD, PAST_LEN + A_PAD + DEC_SEQ)
        ka_s[0:PAST_LEN, :] = ca_ref[0].astype(bf16)
        ka_s[lat_a, :] = kva_ref[:, 0:D_KVH]
        kb_s[0:PAST_LEN, :] = cb_ref[0].astype(bf16)
        kb_s[PAST_LEN:, :] = kvb_ref[:, 0:D_KVH]
        for pad0 in (PAST_LEN, PAST_LEN + A_PAD + DEC_SEQ):
            ka_s[pad0:pad0 + A_PAD, :] = jnp.zeros((A_PAD, D_KVH), bf16)
            va_s[pad0:pad0 + A_PAD, :] = jnp.zeros((A_PAD, 2 * D_KVH), bf16)
        for g in range(N_KV_A):
            src = slice(g * HEAD_DIM, (g + 1) * HEAD_DIM)
            vsrc = slice(D_KVH + g * HEAD_DIM, D_KVH + (g + 1) * HEAD_DIM)
            dst = slice(2 * g * HEAD_DIM, (2 * g + 1) * HEAD_DIM)
            one = slice((2 * g + 1) * HEAD_DIM, (2 * g + 2) * HEAD_DIM)
            va_s[0:PAST_LEN, dst] = ca_ref[1, :, src].astype(bf16)
            va_s[lat_a, dst] = kva_ref[:, vsrc]
            va_s[0:PAST_LEN, one] = jnp.ones((PAST_LEN, HEAD_DIM), bf16)
            va_s[lat_a, one] = jnp.ones((DEC_SEQ, HEAD_DIM), bf16)
            vb_s[0:PAST_LEN, dst] = cb_ref[1, :, src].astype(bf16)
            vb_s[PAST_LEN:, dst] = kvb_ref[:, vsrc]
            vb_s[:, one] = jnp.ones((S_B, HEAD_DIM), bf16)

    win = pl.ds(pl.multiple_of(PAST_LEN + n * BLOCK, BLOCK), 3 * BLOCK)
    bias = jnp.concatenate([bias_ref[...]] * GROUP, axis=0)
    jobs, firsts = [], []
    for g in range(N_KV_A):
        kc = slice(g * HEAD_DIM, (g + 1) * HEAD_DIM)
        vc = slice(2 * g * HEAD_DIM, (2 * g + 2) * HEAD_DIM)
        k = jnp.concatenate([ka_s[0:PAST_LEN, kc], ka_s[win, kc]], axis=0)
        v1 = jnp.concatenate([va_s[0:PAST_LEN, vc], va_s[win, vc]], axis=0)
        sink = _sink_column(sink_ref, layer * N_HEADS_A + g * GROUP, BLOCK)
        jobs.append((_stack_heads(q_ref, g * GROUP), k, v1, bias, sink))
        firsts.append(g * GROUP)

    for g in range(N_KV_B):
        kc = slice(g * HEAD_DIM, (g + 1) * HEAD_DIM)
        first = N_HEADS_A + g * GROUP
        jobs.append((_stack_heads(q_ref, first), kb_s[:, kc],
                     vb_s[:, 2 * g * HEAD_DIM:(2 * g + 2) * HEAD_DIM], None, None))
        firsts.append(first)
    for first, o in zip(firsts, _attend_all(jobs)):
        _store_heads(o_ref, o, first, BLOCK)


def _lat_attn(sink, q, kv, cache_a, cache_b, bias, layer):
    blk0 = M_CTX // BLOCK
    cache_spec = pl.BlockSpec((None, None, 2, PAST_LEN, D_KVH), lambda b, n: (b, layer, 0, 0, 0))
    half = 2 * D_KVH
    seq0 = M_CTX // DEC_SEQ
    return pl.pallas_call(
        functools.partial(_lat_attn_kernel, layer=layer),
        grid=(DEC_BATCH, N_QB),
        in_specs=[
            pl.BlockSpec(memory_space=pltpu.SMEM),
            pl.BlockSpec((BLOCK, D_Q), lambda b, n: (blk0 + b * N_QB + n, 0)),
            pl.BlockSpec((DEC_SEQ, half), lambda b, n: (seq0 + b, 0)),
            pl.BlockSpec((DEC_SEQ, half), lambda b, n: (seq0 + b, 1)),
            cache_spec, cache_spec,
            pl.BlockSpec((None, BLOCK, PAST_LEN + 3 * BLOCK),
                         lambda b, n: (jnp.where(n == 0, 0, jnp.where(n == N_QB - 1, 2, 1)), 0, 0)),
        ],
        out_specs=pl.BlockSpec((BLOCK, D_Q), lambda b, n: (b * N_QB + n, 0)),
        out_shape=jax.ShapeDtypeStruct((M_LAT, D_Q), bf16),
        scratch_shapes=[
            pltpu.VMEM((A_ROWS, D_KVH), bf16), pltpu.VMEM((A_ROWS, 2 * D_KVH), bf16),
            pltpu.VMEM((S_B, D_KVH), bf16), pltpu.VMEM((S_B, 2 * D_KVH), bf16)],
        compiler_params=_params("arbitrary", "arbitrary"),
    )(sink, q, kv, kv, cache_a, cache_b, bias)


def _oproj_kernel(*refs, n_x):
    oc_ref, ol_ref = refs[:2]
    x_refs = refs[2:2 + n_x]
    w_ref, gate_ref, g_ref, b_ref, shift_ref, scale_ref, y_ref, h_ref = refs[2 + n_x:]
    is_ctx = pl.program_id(0) < N_CTX_TILES
    o = jnp.where(is_ctx, oc_ref[...], ol_ref[...])
    x = _row_load(x_refs, is_ctx)
    subs = [slice(r0, r0 + SEQ) for r0 in range(0, TM, SEQ)]
    fs = [jnp.dot(o[rows], w_ref[...], preferred_element_type=f32) for rows in subs]
    for rows, f in zip(subs, fs):
        y = _ln(ALPHA * x[rows] + gate_ref[...] * f) * g_ref[...] + b_ref[...]
        y_ref[rows, :] = y
        h_ref[rows, :] = (_ln(y) * (1.0 + scale_ref[...]) + shift_ref[...]).astype(bf16)


def _oproj(o_ctx, o_lat, xs, w_o, mod, ln_g, ln_b, layer):
    return pl.pallas_call(
        functools.partial(_oproj_kernel, n_x=len(xs)),
        grid=(N_TILES,),
        in_specs=[
            pl.BlockSpec((TM, D_Q), lambda i: (jnp.minimum(i, N_CTX_TILES - 1), 0)),
            pl.BlockSpec((TM, D_Q), lambda i: (jnp.maximum(i - N_CTX_TILES, 0), 0)),
        ] + _row_specs(len(xs) == 2) + [
            pl.BlockSpec((None, D_Q, D_MODEL), lambda i: (layer, 0, 0),
                         pipeline_mode=pl.Buffered(1)),
            _mod_spec(layer, 2), _layer_vec_spec(layer, D_MODEL), _layer_vec_spec(layer, D_MODEL),
            _mod_spec(layer, 3), _mod_spec(layer, 4),
        ],
        out_specs=[
            pl.BlockSpec((TM, D_MODEL), lambda i: (i, 0)),
            pl.BlockSpec((TM, D_MODEL), lambda i: (i, 0)),
        ],
        out_shape=[
            jax.ShapeDtypeStruct((M_ALL, D_MODEL), f32),
            jax.ShapeDtypeStruct((M_ALL, D_MODEL), bf16),
        ],
        compiler_params=_params("arbitrary"),
    )(o_ctx, o_lat, *xs, w_o, mod, ln_g, ln_b, mod, mod)


STEP_ROWS = 2 * TM
STEPS_PER_LAT_SEQ = DEC_SEQ // STEP_ROWS


def _conv_rows(u, cw, cb, is_ctx):
    mid_rows = slice(HALO, HALO + STEP_ROWS)
    um = pltpu.roll(u, 1, 0)[mid_rows]
    up = pltpu.roll(u, u.shape[0] - 1, 0)[mid_rows]
    lo, mid, hi = um * cw[0:1], u[mid_rows] * cw[1:2] + cb, up * cw[2:3]
    out = lo + mid + hi
    pieces, done = [], 0
    for b in range(SEQ, STEP_ROWS, SEQ):
        rows = slice(b - 8, b + 8)
        r = lax.broadcasted_iota(jnp.int32, (16, u.shape[1]), 0) + (b - 8)
        fixed = jnp.where(r == b - 1, lo[rows] + mid[rows],
                          jnp.where(r == b, mid[rows] + hi[rows], out[rows]))
        pieces += [out[done:b - 8], jnp.where(is_ctx, fixed, out[rows])]
        done = b + 8
    return jnp.concatenate(pieces + [out[done:]], axis=0)


def _ffn_up_kernel(hp_ref, h_ref, hn_ref, wv_ref, wg_ref, cwv_ref, cwg_ref, cbv_ref, cbg_ref,
                   wd_ref, a_ref, wdb_ref, w_s, hcat):
    j = pl.program_id(1)

    @pl.when(j == 0)
    def _():
        w_s[:, :TF] = wv_ref[...].astype(bf16)
        w_s[:, TF:] = wg_ref[...].astype(bf16)
        wdb_ref[...] = wd_ref[...].astype(bf16)

    n_ctx = M_CTX // STEP_ROWS
    is_ctx = j < n_ctx
    t = (j - n_ctx) % STEPS_PER_LAT_SEQ
    zero = jnp.zeros((HALO, D_MODEL), bf16)
    hcat[0:HALO, :] = jnp.where(is_ctx | (t == 0), zero, hp_ref[...])
    hcat[HALO:HALO + STEP_ROWS, :] = h_ref[...]
    hcat[HALO + STEP_ROWS:, :] = jnp.where(is_ctx | (t == STEPS_PER_LAT_SEQ - 1), zero, hn_ref[...])
    u = jnp.dot(hcat[...], w_s[...], preferred_element_type=f32)
    val = _conv_rows(u[:, :TF], cwv_ref[...], cbv_ref[...], is_ctx)
    gate = _conv_rows(u[:, TF:], cwg_ref[...], cbg_ref[...], is_ctx)
    a_ref[...] = (gate * jax.nn.sigmoid(gate) * val).astype(bf16)


def _ffn_up(h, w_up, conv_w, conv_b, w_down, layer):
    hb = STEP_ROWS // HALO
    return pl.pallas_call(
        _ffn_up_kernel,
        grid=(N_F, M_ALL // STEP_ROWS),
        in_specs=[
            pl.BlockSpec((HALO, D_MODEL), lambda f, j: (jnp.maximum(j * hb - 1, 0), 0)),
            pl.BlockSpec((STEP_ROWS, D_MODEL), lambda f, j: (j, 0)),
            pl.BlockSpec((HALO, D_MODEL),
                         lambda f, j: (jnp.minimum((j + 1) * hb, M_ALL // HALO - 1), 0)),
            pl.BlockSpec((None, D_MODEL, TF), lambda f, j: (layer, 0, f)),
            pl.BlockSpec((None, D_MODEL, TF), lambda f, j: (layer, 0, N_F + f)),
            pl.BlockSpec((None, 3, TF), lambda f, j: (layer, 0, f)),
            pl.BlockSpec((None, 3, TF), lambda f, j: (layer, 0, N_F + f)),
            pl.BlockSpec((None, 1, TF), lambda f, j: (layer, 0, f)),
            pl.BlockSpec((None, 1, TF), lambda f, j: (layer, 0, N_F + f)),
            pl.BlockSpec((None, TF, D_MODEL), lambda f, j: (layer, f, 0)),
        ],
        out_specs=[pl.BlockSpec((STEP_ROWS, TF), lambda f, j: (j, f)),
                   pl.BlockSpec((TF, D_MODEL), lambda f, j: (f, 0))],
        out_shape=[jax.ShapeDtypeStruct((M_ALL, D_FF), bf16),
                   jax.ShapeDtypeStruct((D_FF, D_MODEL), bf16)],
        scratch_shapes=[pltpu.VMEM((D_MODEL, 2 * TF), bf16),
                        pltpu.VMEM((STEP_ROWS + 2 * HALO, D_MODEL), bf16)],
        compiler_params=_params("arbitrary", "arbitrary"),
    )(h, h, h, w_up, w_up, conv_w, conv_w, conv_b, conv_b, w_down)


def _ffn_down_kernel(a_ref, y_ref, w_ref, gate_ref, g_ref, b_ref, *o_refs):
    f = jnp.dot(a_ref[...], w_ref[...], preferred_element_type=f32)
    out = _ln(ALPHA * y_ref[...] + gate_ref[...] * f) * g_ref[...] + b_ref[...]
    if len(o_refs) == 1:
        o_refs[0][...] = out
    else:
        is_ctx = pl.program_id(0) < M_CTX // TM_DOWN

        @pl.when(is_ctx)
        def _():
            o_refs[0][...] = out

        @pl.when(jnp.logical_not(is_ctx))
        def _():
            o_refs[1][...] = out


def _ffn_down(a, y, w_down, mod, ln_g, ln_b, layer, split_out):
    if split_out:
        out_shape = [jax.ShapeDtypeStruct((M_CTX, D_MODEL), f32),
                     jax.ShapeDtypeStruct((M_LAT, D_MODEL), f32)]
    else:
        out_shape = [jax.ShapeDtypeStruct((M_ALL, D_MODEL), f32)]
    return pl.pallas_call(
        _ffn_down_kernel,
        grid=(M_ALL // TM_DOWN,),
        in_specs=[
            pl.BlockSpec((TM_DOWN, D_FF), lambda i: (i, 0)),
            pl.BlockSpec((TM_DOWN, D_MODEL), lambda i: (i, 0)),
            pl.BlockSpec((D_FF, D_MODEL), lambda i: (0, 0), pipeline_mode=pl.Buffered(1)),
            _mod_spec(layer, 5, tm=TM_DOWN),
            _layer_vec_spec(layer, D_MODEL), _layer_vec_spec(layer, D_MODEL),
        ],
        out_specs=_row_specs(split_out, tm=TM_DOWN),
        out_shape=out_shape,
        compiler_params=_params("arbitrary"),
    )(a, y, w_down, mod, ln_g, ln_b)


def _rope_tables():
    pos = np.arange(DEC_SEQ)
    q4 = HEAD_DIM // 4
    freq = np.float32(ROPE_THETA) ** (-np.arange(q4, dtype=np.float32) / np.float32(q4))
    row = ((pos // GRID_W).astype(np.float32)[:, None] * freq).astype(np.float64)
    col = ((pos % GRID_W).astype(np.float32)[:, None] * freq).astype(np.float64)
    zero = np.zeros_like(row)
    c = np.concatenate([np.cos(row), np.cos(row), np.cos(col), np.cos(col)], axis=1)
    a = np.concatenate([-np.sin(row), zero, -np.sin(col), zero], axis=1)
    b = np.concatenate([zero, np.sin(row), zero, np.sin(col)], axis=1)
    return tuple(jnp.asarray(t, f32) for t in (c, a, b))


def kernel(x_prompt, x_sample, cache_attn_a, cache_attn_b, c, c_ctx, w_ada, b_ada, w_in,
           q_norm_g, k_norm_g, sink_a, w_o, ln1_g, ln1_b, w_up, conv_w, conv_b, w_down,
           ln2_g, ln2_b):
    cvecs = jnp.concatenate(
        [c_ctx[None], c, jnp.zeros((N_MOD_ROWS - 1 - DEC_BATCH, D_MODEL), f32)], axis=0)
    mod = _adaln(cvecs, w_ada, b_ada).reshape(DEPTH * N_MOD_ROWS * 6, 1, D_MODEL)

    w_in_b = w_in.astype(bf16)
    w_o_b = w_o.astype(bf16)
    rope = _rope_tables()
    win_bias = _window_bias()
    sink = sink_a.reshape(DEPTH * N_HEADS_A)
    cache_a = cache_attn_a.reshape(DEC_BATCH, DEPTH, 2, PAST_LEN, D_KVH)
    cache_b = cache_attn_b.reshape(DEC_BATCH, DEPTH, 2, PAST_LEN, D_KVH)
    per_layer = lambda v: v.reshape(DEPTH, 1, v.shape[-1])
    qg, kg = per_layer(q_norm_g), per_layer(k_norm_g)
    g1, b1, g2, b2 = per_layer(ln1_g), per_layer(ln1_b), per_layer(ln2_g), per_layer(ln2_b)
    conv_b3 = per_layer(conv_b)

    xs = [x_prompt.reshape(M_CTX, D_MODEL), x_sample.reshape(M_LAT, D_MODEL)]
    new_caches = []
    for l in range(DEPTH):
        q, kv, new_a, new_b = _qkv(xs, mod, w_in_b, qg, kg, rope, new_caches, l)
        new_caches = [new_a, new_b]
        o_ctx = _ctx_attn(sink, q, kv, l)
        o_lat = _lat_attn(sink, q, kv, cache_a, cache_b, win_bias, l)
        y, h = _oproj(o_ctx, o_lat, xs, w_o_b, mod, g1, b1, l)
        act, w_down_b = _ffn_up(h, w_up, conv_w, conv_b3, w_down, l)
        xs = _ffn_down(act, y, w_down_b, mod, g2, b2, l, split_out=(l == DEPTH - 1))

    cache_shape = (BATCH, DEPTH, 2, SEQ, N_KV_A, HEAD_DIM)
    return (xs[0].reshape(BATCH, SEQ, D_MODEL), xs[1].reshape(DEC_BATCH, DEC_SEQ, D_MODEL),
            new_caches[0].reshape(cache_shape), new_caches[1].reshape(cache_shape))
```

```python
import functools

import jax
import jax.numpy as jnp
import numpy as np
from jax import lax
from jax.experimental import pallas as pl
from jax.experimental.pallas import tpu as pltpu

D_MODEL = 2048
BATCH = 32
SEQ = 256
DEPTH = 2
DEC_BATCH = 2
DEC_SEQ = 2048
PAST_LEN = 256
GRID_W = 64
HEAD_DIM = 128
N_HEADS_A = 8
N_KV_A = 2
N_HEADS_B = 8
N_KV_B = 2
GROUP = 4
BLOCK = 128
D_FF = 5632
ROPE_THETA = 10000.0
LN_EPS = 1e-6
ALPHA = (2.0 * DEPTH) ** 0.25
SCALE = HEAD_DIM ** -0.5
LOG2E = 1.4426950408889634
LN2 = 0.6931471805599453
QSCALE = SCALE * LOG2E
NEG = -1e30

D_Q = (N_HEADS_A + N_HEADS_B) * HEAD_DIM
D_KVH = N_KV_A * HEAD_DIM
D_KV = 4 * D_KVH
D_IN = D_Q + D_KV
OFF_QA = 0
OFF_KA = OFF_QA + N_HEADS_A * HEAD_DIM
OFF_VA = OFF_KA + D_KVH
OFF_QB = OFF_VA + D_KVH
OFF_KB = OFF_QB + N_HEADS_B * HEAD_DIM
OFF_VB = OFF_KB + D_KVH

M_CTX = BATCH * SEQ
M_LAT = DEC_BATCH * DEC_SEQ
M_ALL = M_CTX + M_LAT
N_MOD_ROWS = 8

TM = 2 * SEQ
N_TILES = M_ALL // TM
N_CTX_TILES = M_CTX // TM
TILES_PER_LAT_SEQ = DEC_SEQ // TM
TM_DOWN = 256
HALO = 16
TF = 512
N_F = D_FF // TF
TN_ADA = 2048
F32_ROWS = 8
VMEM_LIMIT = 60 * 1024 * 1024

f32 = jnp.float32
bf16 = jnp.bfloat16


def _params(*sem):
    return pltpu.CompilerParams(dimension_semantics=sem, vmem_limit_bytes=VMEM_LIMIT)


def _ln(x):
    mu = jnp.mean(x, axis=-1, keepdims=True)
    xc = x - mu
    var = jnp.mean(xc * xc, axis=-1, keepdims=True)
    return xc * lax.rsqrt(var + LN_EPS)


def _mod_spec(layer, which, tm=TM):
    n_ctx = M_CTX // tm
    per_seq = DEC_SEQ // tm

    def index_map(i):
        row = jnp.where(i < n_ctx, 0, 1 + (i - n_ctx) // per_seq)
        return ((layer * N_MOD_ROWS + row) * 6 + which, 0, 0)
    return pl.BlockSpec((None, 1, D_MODEL), index_map)


def _layer_vec_spec(layer, width):
    return pl.BlockSpec((None, 1, width), lambda i: (layer, 0, 0))


def _row_specs(split, tm=TM):
    n_ctx = M_CTX // tm
    if not split:
        return [pl.BlockSpec((tm, D_MODEL), lambda i: (i, 0))]
    return [pl.BlockSpec((tm, D_MODEL), lambda i: (jnp.minimum(i, n_ctx - 1), 0)),
            pl.BlockSpec((tm, D_MODEL), lambda i: (jnp.maximum(i - n_ctx, 0), 0))]


def _adaln_kernel(cv_ref, w_ref, b_ref, o_ref):
    cv = cv_ref[...]
    a = (cv * jax.nn.sigmoid(cv)).astype(bf16)
    o_ref[...] = jnp.dot(a, w_ref[...].astype(bf16), preferred_element_type=f32) + b_ref[...]


def _adaln(cvecs, w_ada, b_ada):
    return pl.pallas_call(
        _adaln_kernel,
        grid=(DEPTH, 6 * D_MODEL // TN_ADA),
        in_specs=[
            pl.BlockSpec((N_MOD_ROWS, D_MODEL), lambda l, n: (0, 0)),
            pl.BlockSpec((None, D_MODEL, TN_ADA), lambda l, n: (l, 0, n)),
            pl.BlockSpec((None, 1, TN_ADA), lambda l, n: (l, 0, n)),
        ],
        out_specs=pl.BlockSpec((None, N_MOD_ROWS, TN_ADA), lambda l, n: (l, 0, n)),
        out_shape=jax.ShapeDtypeStruct((DEPTH, N_MOD_ROWS, 6 * D_MODEL), f32),
        compiler_params=_params("arbitrary", "arbitrary"),
    )(cvecs, w_ada, b_ada.reshape(DEPTH, 1, 6 * D_MODEL))


def _qkv_kernel(*refs, n_x, n_alias):
    x_refs = refs[:n_x]
    (shift_ref, scale_ref, w_ref, qg_ref, kg_ref, rc_ref, ra_ref, rb_ref) = refs[n_x:n_x + 8]
    q_ref, kv_ref, ca_ref, cb_ref = refs[n_x + 8 + n_alias:]
    i = pl.program_id(0)
    is_ctx = i < N_CTX_TILES

    def head(qkv, rows, col, gain, rope):
        xh = qkv[:, col:col + HEAD_DIM]
        if gain is not None:
            ms = jnp.mean(xh * xh, axis=-1, keepdims=True)
            xh = xh * lax.rsqrt(ms + LN_EPS) * gain
        if rope:
            xh = (xh * rc_ref[rows, :] + pltpu.roll(xh, HEAD_DIM - 32, 1) * ra_ref[rows, :]
                  + pltpu.roll(xh, 32, 1) * rb_ref[rows, :])
        return xh

    def emit_rows(qkv, s, rope):
        rows = slice(s * SEQ, (s + 1) * SEQ)
        qg = qg_ref[...]
        kg = kg_ref[...]
        for hh in range(N_HEADS_A):
            q_ref[rows, hh * HEAD_DIM:(hh + 1) * HEAD_DIM] = (
                head(qkv, rows, OFF_QA + hh * HEAD_DIM, None, rope) * QSCALE).astype(bf16)
        for hh in range(N_HEADS_B):
            c0 = (N_HEADS_A + hh) * HEAD_DIM
            q_ref[rows, c0:c0 + HEAD_DIM] = (
                head(qkv, rows, OFF_QB + hh * HEAD_DIM, qg, rope) * QSCALE).astype(bf16)
        va = qkv[:, OFF_VA:OFF_VA + D_KVH]
        vb = qkv[:, OFF_VB:OFF_VB + D_KVH]
        kv_ref[rows, D_KVH:2 * D_KVH] = va.astype(bf16)
        kv_ref[rows, 3 * D_KVH:] = vb.astype(bf16)
        for j in range(N_KV_A):
            cols = slice(j * HEAD_DIM, (j + 1) * HEAD_DIM)
            ka = head(qkv, rows, OFF_KA + j * HEAD_DIM, None, rope)
            kb = head(qkv, rows, OFF_KB + j * HEAD_DIM, kg, rope)
            kv_ref[rows, cols] = ka.astype(bf16)
            kv_ref[rows, 2 * D_KVH + j * HEAD_DIM:2 * D_KVH + (j + 1) * HEAD_DIM] = (
                kb.astype(bf16))
            if not rope:
                dst = pl.ds(j, SEQ, stride=N_KV_A)
                ca_ref[s, 0, dst, :] = ka
                ca_ref[s, 1, dst, :] = va[:, cols]
                cb_ref[s, 0, dst, :] = kb
                cb_ref[s, 1, dst, :] = vb[:, cols]

    def emit(rope):
        hs = []
        for s in range(TM // SEQ):
            x = x_refs[-1 if rope else 0][s * SEQ:(s + 1) * SEQ, :]
            hs.append((_ln(x) * (1.0 + scale_ref[...]) + shift_ref[...]).astype(bf16))
        qkvs = [jnp.dot(h, w_ref[...], preferred_element_type=f32) for h in hs]
        for s, qkv in enumerate(qkvs):
            emit_rows(qkv, s, rope)

    @pl.when(is_ctx)
    def _():
        emit(False)
        if n_alias == 0:
            rest = ca_ref.shape[1] - 2
            for c_ref in (ca_ref, cb_ref):
                c_ref[:, 2:] = jnp.zeros((TM // SEQ, rest, SEQ * N_KV_A, HEAD_DIM), f32)

    @pl.when(jnp.logical_not(is_ctx))
    def _():
        emit(True)


def _qkv(xs, mod, w_in, qg, kg, rope, caches, layer):
    rope_spec = pl.BlockSpec(
        (TM, HEAD_DIM), lambda i: (jnp.maximum(i - N_CTX_TILES, 0) % TILES_PER_LAT_SEQ, 0))
    assert caches or layer == 0
    slots, slot0 = (2 * DEPTH, 0) if not caches else (2, layer)
    cache_spec = pl.BlockSpec((TM // SEQ, slots, SEQ * N_KV_A, HEAD_DIM),
                              lambda i: (jnp.minimum(i, N_CTX_TILES - 1), slot0, 0, 0))
    cache_shape = jax.ShapeDtypeStruct((BATCH, DEPTH * 2, SEQ * N_KV_A, HEAD_DIM), f32)
    n_in = len(xs) + 8
    return pl.pallas_call(
        functools.partial(_qkv_kernel, n_x=len(xs), n_alias=len(caches)),
        grid=(N_TILES,),
        in_specs=_row_specs(len(xs) == 2) + [
            _mod_spec(layer, 0), _mod_spec(layer, 1),
            pl.BlockSpec((D_MODEL, D_IN), lambda i: (0, 0), pipeline_mode=pl.Buffered(1)),
            _layer_vec_spec(layer, HEAD_DIM), _layer_vec_spec(layer, HEAD_DIM),
            rope_spec, rope_spec, rope_spec,
        ] + [pl.BlockSpec(memory_space=pl.ANY)] * len(caches),
        out_specs=[
            pl.BlockSpec((TM, D_Q), lambda i: (i, 0)),
            pl.BlockSpec((TM, D_KV), lambda i: (i, 0)),
            cache_spec, cache_spec,
        ],
        out_shape=[
            jax.ShapeDtypeStruct((M_ALL, D_Q), bf16),
            jax.ShapeDtypeStruct((M_ALL, D_KV), bf16),
            cache_shape, cache_shape,
        ],
        input_output_aliases={n_in + k: 2 + k for k in range(len(caches))},
        compiler_params=_params("arbitrary"),
    )(*xs, mod, mod, w_in, qg, kg, *rope, *caches)


def _stack_heads(q_ref, first_head):
    return jnp.concatenate(
        [q_ref[:, (first_head + j) * HEAD_DIM:(first_head + j + 1) * HEAD_DIM]
         for j in range(GROUP)], axis=0)


def _sink_column(sink_ref, first, rows):
    return jnp.concatenate(
        [jnp.full((rows, 1), sink_ref[first + j] * LOG2E, f32) for j in range(GROUP)], axis=0)


def _logits(qs, k):
    return lax.dot_general(qs, k, (((1,), (1,)), ((), ())), preferred_element_type=f32)


def _with_ones(v):
    return jnp.concatenate([v, jnp.ones_like(v)], axis=1)


def _attend_all(jobs):
    ss = []
    for qs, k, _, bias, _ in jobs:
        s = _logits(qs, k)
        ss.append(s if bias is None else s + bias)
    ms = []
    for s, (_, _, _, _, sink) in zip(ss, jobs):
        m = jnp.max(s, axis=-1, keepdims=True)
        ms.append(m if sink is None else jnp.maximum(m, sink))
    ps = [jnp.exp2(s - m).astype(bf16) for s, m in zip(ss, ms)]
    outs = []
    for p, m, (_, _, v1, _, sink) in zip(ps, ms, jobs):
        o = jnp.dot(p, v1, preferred_element_type=f32)
        l = o[:, HEAD_DIM:]
        if sink is not None:
            l = l + jnp.exp2(sink - m)
        outs.append(o[:, :HEAD_DIM] * (1.0 / l))
    return outs


def _store_heads(o_ref, o, first_head, rows):
    for j in range(GROUP):
        c0 = (first_head + j) * HEAD_DIM
        o_ref[:, c0:c0 + HEAD_DIM] = o[j * rows:(j + 1) * rows].astype(o_ref.dtype)


def _ctx_attn_kernel(sink_ref, q_ref, kv_ref, *refs, layer):
    o_ref = refs[-1] if len(refs) == 1 else refs[2]
    if len(refs) > 1:
        win_ref, wo_ref, _, winb_ref, wob_ref = refs
        winb_ref[...] = win_ref[...].astype(bf16)
        wob_ref[...] = wo_ref[...].astype(bf16)
    for mixer in range(2):
        for g in range(2):
            kcol = mixer * 2 * D_KVH + g * HEAD_DIM
            k = kv_ref[:, kcol:kcol + HEAD_DIM]
            v = kv_ref[:, kcol + D_KVH:kcol + D_KVH + HEAD_DIM]
            first = mixer * N_HEADS_A + g * GROUP
            sink = None
            if mixer == 0:
                sink = _sink_column(sink_ref, layer * N_HEADS_A + g * GROUP, SEQ)
            o, = _attend_all([(_stack_heads(q_ref, first), k, _with_ones(v), None, sink)])
            _store_heads(o_ref, o, first, SEQ)


def _ctx_attn(sink, q, kv, layer, next_weights=()):
    in_specs = [
        pl.BlockSpec(memory_space=pltpu.SMEM),
        pl.BlockSpec((SEQ, D_Q), lambda b: (b, 0)),
        pl.BlockSpec((SEQ, D_KV), lambda b: (b, 0)),
    ]
    out_specs = [pl.BlockSpec((SEQ, D_Q), lambda b: (b, 0))]
    out_shape = [jax.ShapeDtypeStruct((M_CTX, D_Q), bf16)]
    for w in next_weights:
        rows, cols = w.shape[1] // BATCH, w.shape[2]
        in_specs.append(pl.BlockSpec((None, rows, cols), lambda b: (layer + 1, b, 0)))
        out_specs.append(pl.BlockSpec((rows, cols), lambda b: (b, 0)))
        out_shape.append(jax.ShapeDtypeStruct(w.shape[1:], bf16))
    return pl.pallas_call(
        functools.partial(_ctx_attn_kernel, layer=layer),
        grid=(BATCH,),
        in_specs=in_specs,
        out_specs=out_specs,
        out_shape=out_shape,
        compiler_params=_params("arbitrary"),
    )(sink, q, kv, *next_weights)


QB = 2 * BLOCK
N_QB = DEC_SEQ // QB
WIN = QB + 2 * BLOCK
S_B = PAST_LEN + DEC_SEQ
A_PAD = BLOCK
A_ROWS = PAST_LEN + A_PAD + DEC_SEQ + A_PAD


def _window_bias():
    a = np.arange(QB)[:, None]
    j = np.arange(WIN)[None, :]
    near = (j >= a) & (j <= a + 2 * BLOCK)
    out = []
    for lo, hi in ((BLOCK, WIN), (0, WIN), (0, WIN - BLOCK)):
        ok = near & (j >= lo) & (j < hi)
        out.append(np.concatenate(
            [np.zeros((QB, PAST_LEN), np.float32), np.where(ok, 0.0, NEG).astype(np.float32)],
            axis=1))
    return jnp.asarray(np.stack(out))


def _lat_attn_kernel(sink_ref, q_ref, kva_ref, kvb_ref, ca_ref, cb_ref, bias_ref,
                     o_ref, ka_s, va_s, kb_s, vb_s, *, layer):
    n = pl.program_id(1)

    @pl.when(n == 0)
    def _():
        lat_a = slice(PAST_LEN + A_PAD, PAST_LEN + A_PAD + DEC_SEQ)
        ka_s[0:PAST_LEN, :] = ca_ref[0].astype(bf16)
        ka_s[lat_a, :] = kva_ref[:, 0:D_KVH]
        kb_s[0:PAST_LEN, :] = cb_ref[0].astype(bf16)
        kb_s[PAST_LEN:, :] = kvb_ref[:, 0:D_KVH]
        for pad0 in (PAST_LEN, PAST_LEN + A_PAD + DEC_SEQ):
            ka_s[pad0:pad0 + A_PAD, :] = jnp.zeros((A_PAD, D_KVH), bf16)
            va_s[pad0:pad0 + A_PAD, :] = jnp.zeros((A_PAD, 2 * D_KVH), bf16)
        for g in range(N_KV_A):
            src = slice(g * HEAD_DIM, (g + 1) * HEAD_DIM)
            vsrc = slice(D_KVH + g * HEAD_DIM, D_KVH + (g + 1) * HEAD_DIM)
            dst = slice(2 * g * HEAD_DIM, (2 * g + 1) * HEAD_DIM)
            one = slice((2 * g + 1) * HEAD_DIM, (2 * g + 2) * HEAD_DIM)
            va_s[0:PAST_LEN, dst] = ca_ref[1, :, src].astype(bf16)
            va_s[lat_a, dst] = kva_ref[:, vsrc]
            va_s[0:PAST_LEN, one] = jnp.ones((PAST_LEN, HEAD_DIM), bf16)
            va_s[lat_a, one] = jnp.ones((DEC_SEQ, HEAD_DIM), bf16)
            vb_s[0:PAST_LEN, dst] = cb_ref[1, :, src].astype(bf16)
            vb_s[PAST_LEN:, dst] = kvb_ref[:, vsrc]
            vb_s[:, one] = jnp.ones((S_B, HEAD_DIM), bf16)

    win = pl.ds(pl.multiple_of(PAST_LEN + n * QB, BLOCK), WIN)
    bias = jnp.concatenate([bias_ref[...]] * GROUP, axis=0)
    jobs, firsts = [], []
    for g in range(N_KV_A):
        kc = slice(g * HEAD_DIM, (g + 1) * HEAD_DIM)
        vc = slice(2 * g * HEAD_DIM, (2 * g + 2) * HEAD_DIM)
        k = jnp.concatenate([ka_s[0:PAST_LEN, kc], ka_s[win, kc]], axis=0)
        v1 = jnp.concatenate([va_s[0:PAST_LEN, vc], va_s[win, vc]], axis=0)
        sink = _sink_column(sink_ref, layer * N_HEADS_A + g * GROUP, QB)
        jobs.append((_stack_heads(q_ref, g * GROUP), k, v1, bias, sink))
        firsts.append(g * GROUP)

    for g in range(N_KV_B):
        kc = slice(g * HEAD_DIM, (g + 1) * HEAD_DIM)
        first = N_HEADS_A + g * GROUP
        jobs.append((_stack_heads(q_ref, first), kb_s[:, kc],
                     vb_s[:, 2 * g * HEAD_DIM:(2 * g + 2) * HEAD_DIM], None, None))
        firsts.append(first)
    for first, o in zip(firsts, _attend_all(jobs)):
        _store_heads(o_ref, o, first, QB)


def _lat_attn(sink, q, kv, cache_a, cache_b, bias, layer):
    blk0 = M_CTX // QB
    cache_spec = pl.BlockSpec((None, None, 2, PAST_LEN, D_KVH), lambda b, n: (b, layer, 0, 0, 0))
    half = 2 * D_KVH
    seq0 = M_CTX // DEC_SEQ
    return pl.pallas_call(
        functools.partial(_lat_attn_kernel, layer=layer),
        grid=(DEC_BATCH, N_QB),
        in_specs=[
            pl.BlockSpec(memory_space=pltpu.SMEM),
            pl.BlockSpec((QB, D_Q), lambda b, n: (blk0 + b * N_QB + n, 0)),
            pl.BlockSpec((DEC_SEQ, half), lambda b, n: (seq0 + b, 0)),
            pl.BlockSpec((DEC_SEQ, half), lambda b, n: (seq0 + b, 1)),
            cache_spec, cache_spec,
            pl.BlockSpec((None, QB, PAST_LEN + WIN),
                         lambda b, n: (jnp.where(n == 0, 0, jnp.where(n == N_QB - 1, 2, 1)), 0, 0)),
        ],
        out_specs=pl.BlockSpec((QB, D_Q), lambda b, n: (b * N_QB + n, 0)),
        out_shape=jax.ShapeDtypeStruct((M_LAT, D_Q), bf16),
        scratch_shapes=[
            pltpu.VMEM((A_ROWS, D_KVH), bf16), pltpu.VMEM((A_ROWS, 2 * D_KVH), bf16),
            pltpu.VMEM((S_B, D_KVH), bf16), pltpu.VMEM((S_B, 2 * D_KVH), bf16)],
        compiler_params=_params("arbitrary", "arbitrary"),
    )(sink, q, kv, kv, cache_a, cache_b, bias)


def _oproj_kernel(*refs, n_x):
    oc_ref, ol_ref = refs[:2]
    x_refs = refs[2:2 + n_x]
    w_ref, gate_ref, g_ref, b_ref, shift_ref, scale_ref, y_ref, h_ref = refs[2 + n_x:]
    is_ctx = pl.program_id(0) < N_CTX_TILES

    def body(o_ref, x_ref):
        subs = [slice(r0, r0 + SEQ) for r0 in range(0, TM, SEQ)]
        fs = [jnp.dot(o_ref[rows, :], w_ref[...], preferred_element_type=f32) for rows in subs]
        for rows, f in zip(subs, fs):
            y = _ln(ALPHA * x_ref[rows, :] + gate_ref[...] * f) * g_ref[...] + b_ref[...]
            y_ref[rows, :] = y
            h_ref[rows, :] = (_ln(y) * (1.0 + scale_ref[...]) + shift_ref[...]).astype(bf16)

    @pl.when(is_ctx)
    def _():
        body(oc_ref, x_refs[0])

    @pl.when(jnp.logical_not(is_ctx))
    def _():
        body(ol_ref, x_refs[-1])


def _oproj(o_ctx, o_lat, xs, w_o, mod, ln_g, ln_b, layer):
    return pl.pallas_call(
        functools.partial(_oproj_kernel, n_x=len(xs)),
        grid=(N_TILES,),
        in_specs=[
            pl.BlockSpec((TM, D_Q), lambda i: (jnp.minimum(i, N_CTX_TILES - 1), 0)),
            pl.BlockSpec((TM, D_Q), lambda i: (jnp.maximum(i - N_CTX_TILES, 0), 0)),
        ] + _row_specs(len(xs) == 2) + [
            pl.BlockSpec((D_Q, D_MODEL), lambda i: (0, 0), pipeline_mode=pl.Buffered(1)),
            _mod_spec(layer, 2), _layer_vec_spec(layer, D_MODEL), _layer_vec_spec(layer, D_MODEL),
            _mod_spec(layer, 3), _mod_spec(layer, 4),
        ],
        out_specs=[
            pl.BlockSpec((TM, D_MODEL), lambda i: (i, 0)),
            pl.BlockSpec((TM, D_MODEL), lambda i: (i, 0)),
        ],
        out_shape=[
            jax.ShapeDtypeStruct((M_ALL, D_MODEL), f32),
            jax.ShapeDtypeStruct((M_ALL, D_MODEL), bf16),
        ],
        compiler_params=_params("arbitrary"),
    )(o_ctx, o_lat, *xs, w_o, mod, ln_g, ln_b, mod, mod)


STEP_ROWS = 2 * TM
STEPS_PER_LAT_SEQ = DEC_SEQ // STEP_ROWS


def _conv_rows(u, cw, cb, is_ctx):
    mid_rows = slice(HALO, HALO + STEP_ROWS)
    um = pltpu.roll(u, 1, 0)[mid_rows]
    up = pltpu.roll(u, u.shape[0] - 1, 0)[mid_rows]
    lo, mid, hi = um * cw[0:1], u[mid_rows] * cw[1:2] + cb, up * cw[2:3]
    out = lo + mid + hi
    pieces, done = [], 0
    for b in range(SEQ, STEP_ROWS, SEQ):
        rows = slice(b - F32_ROWS, b + F32_ROWS)
        r = lax.broadcasted_iota(jnp.int32, (2 * F32_ROWS, u.shape[1]), 0) + (b - F32_ROWS)
        fixed = jnp.where(r == b - 1, lo[rows] + mid[rows],
                          jnp.where(r == b, mid[rows] + hi[rows], out[rows]))
        pieces += [out[done:b - F32_ROWS], jnp.where(is_ctx, fixed, out[rows])]
        done = b + F32_ROWS
    return jnp.concatenate(pieces + [out[done:]], axis=0)


def _ffn_up_kernel(hp_ref, h_ref, hn_ref, wv_ref, wg_ref, cwv_ref, cwg_ref, cbv_ref, cbg_ref,
                   wd_ref, a_ref, wdb_ref, w_s, hcat):
    j = pl.program_id(1)

    @pl.when(j == 0)
    def _():
        w_s[:, :TF] = wv_ref[...].astype(bf16)
        w_s[:, TF:] = wg_ref[...].astype(bf16)
        wdb_ref[...] = wd_ref[...].astype(bf16)

    n_ctx = M_CTX // STEP_ROWS
    is_ctx = j < n_ctx
    t = (j - n_ctx) % STEPS_PER_LAT_SEQ
    zero = jnp.zeros((HALO, D_MODEL), bf16)
    hcat[0:HALO, :] = jnp.where(is_ctx | (t == 0), zero, hp_ref[...])
    hcat[HALO:HALO + STEP_ROWS, :] = h_ref[...]
    hcat[HALO + STEP_ROWS:, :] = jnp.where(is_ctx | (t == STEPS_PER_LAT_SEQ - 1), zero, hn_ref[...])
    u = jnp.dot(hcat[...], w_s[...], preferred_element_type=f32)
    vs = _conv_rows(u[:, :TF], cwv_ref[...] * -LN2, cbv_ref[...] * -LN2, is_ctx)
    gs = _conv_rows(u[:, TF:], cwg_ref[...] * -LOG2E, cbg_ref[...] * -LOG2E, is_ctx)
    a_ref[...] = (gs * vs * (1.0 / (1.0 + jnp.exp2(gs)))).astype(bf16)


def _ffn_up(h, w_up, conv_w, conv_b, w_down, layer):
    hb = STEP_ROWS // HALO
    return pl.pallas_call(
        _ffn_up_kernel,
        grid=(N_F, M_ALL // STEP_ROWS),
        in_specs=[
            pl.BlockSpec((HALO, D_MODEL), lambda f, j: (jnp.maximum(j * hb - 1, 0), 0)),
            pl.BlockSpec((STEP_ROWS, D_MODEL), lambda f, j: (j, 0)),
            pl.BlockSpec((HALO, D_MODEL),
                         lambda f, j: (jnp.minimum((j + 1) * hb, M_ALL // HALO - 1), 0)),
            pl.BlockSpec((None, D_MODEL, TF), lambda f, j: (layer, 0, f)),
            pl.BlockSpec((None, D_MODEL, TF), lambda f, j: (layer, 0, N_F + f)),
            pl.BlockSpec((None, 3, TF), lambda f, j: (layer, 0, f)),
            pl.BlockSpec((None, 3, TF), lambda f, j: (layer, 0, N_F + f)),
            pl.BlockSpec((None, 1, TF), lambda f, j: (layer, 0, f)),
            pl.BlockSpec((None, 1, TF), lambda f, j: (layer, 0, N_F + f)),
            pl.BlockSpec((None, TF, D_MODEL), lambda f, j: (layer, f, 0)),
        ],
        out_specs=[pl.BlockSpec((STEP_ROWS, TF), lambda f, j: (j, f)),
                   pl.BlockSpec((TF, D_MODEL), lambda f, j: (f, 0))],
        out_shape=[jax.ShapeDtypeStruct((M_ALL, D_FF), bf16),
                   jax.ShapeDtypeStruct((D_FF, D_MODEL), bf16)],
        scratch_shapes=[pltpu.VMEM((D_MODEL, 2 * TF), bf16),
                        pltpu.VMEM((STEP_ROWS + 2 * HALO, D_MODEL), bf16)],
        compiler_params=_params("arbitrary", "arbitrary"),
    )(h, h, h, w_up, w_up, conv_w, conv_w, conv_b, conv_b, w_down)


def _ffn_down_kernel(a_ref, y_ref, w_ref, gate_ref, g_ref, b_ref, *o_refs):
    f = jnp.dot(a_ref[...], w_ref[...], preferred_element_type=f32)
    out = _ln(ALPHA * y_ref[...] + gate_ref[...] * f) * g_ref[...] + b_ref[...]
    if len(o_refs) == 1:
        o_refs[0][...] = out
    else:
        is_ctx = pl.program_id(0) < M_CTX // TM_DOWN

        @pl.when(is_ctx)
        def _():
            o_refs[0][...] = out

        @pl.when(jnp.logical_not(is_ctx))
        def _():
            o_refs[1][...] = out


def _ffn_down(a, y, w_down, mod, ln_g, ln_b, layer, split_out):
    if split_out:
        out_shape = [jax.ShapeDtypeStruct((M_CTX, D_MODEL), f32),
                     jax.ShapeDtypeStruct((M_LAT, D_MODEL), f32)]
    else:
        out_shape = [jax.ShapeDtypeStruct((M_ALL, D_MODEL), f32)]
    return pl.pallas_call(
        _ffn_down_kernel,
        grid=(M_ALL // TM_DOWN,),
        in_specs=[
            pl.BlockSpec((TM_DOWN, D_FF), lambda i: (i, 0)),
            pl.BlockSpec((TM_DOWN, D_MODEL), lambda i: (i, 0)),
            pl.BlockSpec((D_FF, D_MODEL), lambda i: (0, 0), pipeline_mode=pl.Buffered(1)),
            _mod_spec(layer, 5, tm=TM_DOWN),
            _layer_vec_spec(layer, D_MODEL), _layer_vec_spec(layer, D_MODEL),
        ],
        out_specs=_row_specs(split_out, tm=TM_DOWN),
        out_shape=out_shape,
        compiler_params=_params("arbitrary"),
    )(a, y, w_down, mod, ln_g, ln_b)


def _rope_tables():
    pos = np.arange(DEC_SEQ)
    q4 = HEAD_DIM // 4
    freq = np.float32(ROPE_THETA) ** (-np.arange(q4, dtype=np.float32) / np.float32(q4))
    row = ((pos // GRID_W).astype(np.float32)[:, None] * freq).astype(np.float64)
    col = ((pos % GRID_W).astype(np.float32)[:, None] * freq).astype(np.float64)
    zero = np.zeros_like(row)
    c = np.concatenate([np.cos(row), np.cos(row), np.cos(col), np.cos(col)], axis=1)
    a = np.concatenate([-np.sin(row), zero, -np.sin(col), zero], axis=1)
    b = np.concatenate([zero, np.sin(row), zero, np.sin(col)], axis=1)
    return tuple(jnp.asarray(t, f32) for t in (c, a, b))


def kernel(x_prompt, x_sample, cache_attn_a, cache_attn_b, c, c_ctx, w_ada, b_ada, w_in,
           q_norm_g, k_norm_g, sink_a, w_o, ln1_g, ln1_b, w_up, conv_w, conv_b, w_down,
           ln2_g, ln2_b):
    cvecs = jnp.concatenate(
        [c_ctx[None], c, jnp.zeros((N_MOD_ROWS - 1 - DEC_BATCH, D_MODEL), f32)], axis=0)
    mod = _adaln(cvecs, w_ada, b_ada).reshape(DEPTH * N_MOD_ROWS * 6, 1, D_MODEL)

    w_in_b = w_in[0].astype(bf16)
    w_o_b = w_o[0].astype(bf16)
    rope = _rope_tables()
    win_bias = _window_bias()
    sink = sink_a.reshape(DEPTH * N_HEADS_A)
    cache_a = cache_attn_a.reshape(DEC_BATCH, DEPTH, 2, PAST_LEN, D_KVH)
    cache_b = cache_attn_b.reshape(DEC_BATCH, DEPTH, 2, PAST_LEN, D_KVH)
    per_layer = lambda v: v.reshape(DEPTH, 1, v.shape[-1])
    qg, kg = per_layer(q_norm_g), per_layer(k_norm_g)
    g1, b1, g2, b2 = per_layer(ln1_g), per_layer(ln1_b), per_layer(ln2_g), per_layer(ln2_b)
    conv_b3 = per_layer(conv_b)

    xs = [x_prompt.reshape(M_CTX, D_MODEL), x_sample.reshape(M_LAT, D_MODEL)]
    new_caches = []
    for l in range(DEPTH):
        q, kv, new_a, new_b = _qkv(xs, mod, w_in_b, qg, kg, rope, new_caches, l)
        new_caches = [new_a, new_b]
        o_ctx, *next_w = _ctx_attn(sink, q, kv, l, (w_in, w_o) if l + 1 < DEPTH else ())
        o_lat = _lat_attn(sink, q, kv, cache_a, cache_b, win_bias, l)
        y, h = _oproj(o_ctx, o_lat, xs, w_o_b, mod, g1, b1, l)
        act, w_down_b = _ffn_up(h, w_up, conv_w, conv_b3, w_down, l)
        xs = _ffn_down(act, y, w_down_b, mod, g2, b2, l, split_out=(l == DEPTH - 1))
        if next_w:
            w_in_b, w_o_b = next_w

    cache_shape = (BATCH, DEPTH, 2, SEQ, N_KV_A, HEAD_DIM)
    return (xs[0].reshape(BATCH, SEQ, D_MODEL), xs[1].reshape(DEC_BATCH, DEC_SEQ, D_MODEL),
            new_caches[0].reshape(cache_shape), new_caches[1].reshape(cache_shape))
```

```python
import functools

import jax
import jax.numpy as jnp
import numpy as np
from jax import lax
from jax.experimental import pallas as pl
from jax.experimental.pallas import tpu as pltpu

D_MODEL = 2048
BATCH = 32
SEQ = 256
DEPTH = 2
DEC_BATCH = 2
DEC_SEQ = 2048
PAST_LEN = 256
GRID_W = 64
HEAD_DIM = 128
N_HEADS_A = 8
N_KV_A = 2
N_HEADS_B = 8
N_KV_B = 2
GROUP = 4
BLOCK = 128
D_FF = 5632
ROPE_THETA = 10000.0
LN_EPS = 1e-6
ALPHA = (2.0 * DEPTH) ** 0.25
SCALE = HEAD_DIM ** -0.5
LOG2E = 1.4426950408889634
LN2 = 0.6931471805599453
QSCALE = SCALE * LOG2E
NEG = -1e30

D_Q = (N_HEADS_A + N_HEADS_B) * HEAD_DIM
D_KVH = N_KV_A * HEAD_DIM
D_KV = 4 * D_KVH
D_IN = D_Q + D_KV
OFF_QA = 0
OFF_KA = OFF_QA + N_HEADS_A * HEAD_DIM
OFF_VA = OFF_KA + D_KVH
OFF_QB = OFF_VA + D_KVH
OFF_KB = OFF_QB + N_HEADS_B * HEAD_DIM
OFF_VB = OFF_KB + D_KVH

M_CTX = BATCH * SEQ
M_LAT = DEC_BATCH * DEC_SEQ
M_ALL = M_CTX + M_LAT
N_MOD_ROWS = 8

TM = 2 * SEQ
N_TILES = M_ALL // TM
N_CTX_TILES = M_CTX // TM
TILES_PER_LAT_SEQ = DEC_SEQ // TM
TM_DOWN = 256
HALO = 16
TF = 512
N_F = D_FF // TF
TN_ADA = 2048
F32_ROWS = 8
VMEM_LIMIT = 60 * 1024 * 1024

f32 = jnp.float32
bf16 = jnp.bfloat16


def _params(*sem):
    return pltpu.CompilerParams(dimension_semantics=sem, vmem_limit_bytes=VMEM_LIMIT)


def _ln(x):
    mu = jnp.mean(x, axis=-1, keepdims=True)
    xc = x - mu
    var = jnp.mean(xc * xc, axis=-1, keepdims=True)
    return xc * lax.rsqrt(var + LN_EPS)


def _mod_spec(layer, which, tm=TM):
    n_ctx = M_CTX // tm
    per_seq = DEC_SEQ // tm

    def index_map(i):
        row = jnp.where(i < n_ctx, 0, 1 + (i - n_ctx) // per_seq)
        return ((layer * N_MOD_ROWS + row) * 6 + which, 0, 0)
    return pl.BlockSpec((None, 1, D_MODEL), index_map)


def _layer_vec_spec(layer, width):
    return pl.BlockSpec((None, 1, width), lambda i: (layer, 0, 0))


def _row_specs(split, tm=TM):
    n_ctx = M_CTX // tm
    if not split:
        return [pl.BlockSpec((tm, D_MODEL), lambda i: (i, 0))]
    return [pl.BlockSpec((tm, D_MODEL), lambda i: (jnp.minimum(i, n_ctx - 1), 0)),
            pl.BlockSpec((tm, D_MODEL), lambda i: (jnp.maximum(i - n_ctx, 0), 0))]


def _adaln_kernel(cv_ref, w_ref, b_ref, o_ref):
    cv = cv_ref[...]
    a = (cv * jax.nn.sigmoid(cv)).astype(bf16)
    o_ref[...] = jnp.dot(a, w_ref[...].astype(bf16), preferred_element_type=f32) + b_ref[...]


def _adaln(cvecs, w_ada, b_ada):
    return pl.pallas_call(
        _adaln_kernel,
        grid=(DEPTH, 6 * D_MODEL // TN_ADA),
        in_specs=[
            pl.BlockSpec((N_MOD_ROWS, D_MODEL), lambda l, n: (0, 0)),
            pl.BlockSpec((None, D_MODEL, TN_ADA), lambda l, n: (l, 0, n)),
            pl.BlockSpec((None, 1, TN_ADA), lambda l, n: (l, 0, n)),
        ],
        out_specs=pl.BlockSpec((None, N_MOD_ROWS, TN_ADA), lambda l, n: (l, 0, n)),
        out_shape=jax.ShapeDtypeStruct((DEPTH, N_MOD_ROWS, 6 * D_MODEL), f32),
        compiler_params=_params("arbitrary", "arbitrary"),
    )(cvecs, w_ada, b_ada.reshape(DEPTH, 1, 6 * D_MODEL))


def _qkv_kernel(*refs, n_x, n_alias):
    x_refs = refs[:n_x]
    (shift_ref, scale_ref, w_ref, qg_ref, kg_ref, rc_ref, ra_ref, rb_ref) = refs[n_x:n_x + 8]
    q_ref, kv_ref, ca_ref, cb_ref = refs[n_x + 8 + n_alias:]
    i = pl.program_id(0)
    is_ctx = i < N_CTX_TILES

    def head(qkv, rows, col, gain, rope):
        xh = qkv[:, col:col + HEAD_DIM]
        if gain is not None:
            ms = jnp.mean(xh * xh, axis=-1, keepdims=True)
            xh = xh * lax.rsqrt(ms + LN_EPS) * gain
        if rope:
            xh = (xh * rc_ref[rows, :] + pltpu.roll(xh, HEAD_DIM - 32, 1) * ra_ref[rows, :]
                  + pltpu.roll(xh, 32, 1) * rb_ref[rows, :])
        return xh

    def emit_rows(qkv, s, rope):
        rows = slice(s * SEQ, (s + 1) * SEQ)
        qg = qg_ref[...]
        kg = kg_ref[...]
        for hh in range(N_HEADS_A):
            q_ref[rows, hh * HEAD_DIM:(hh + 1) * HEAD_DIM] = (
                head(qkv, rows, OFF_QA + hh * HEAD_DIM, None, rope) * QSCALE).astype(bf16)
        for hh in range(N_HEADS_B):
            c0 = (N_HEADS_A + hh) * HEAD_DIM
            q_ref[rows, c0:c0 + HEAD_DIM] = (
                head(qkv, rows, OFF_QB + hh * HEAD_DIM, qg, rope) * QSCALE).astype(bf16)
        va = qkv[:, OFF_VA:OFF_VA + D_KVH]
        vb = qkv[:, OFF_VB:OFF_VB + D_KVH]
        kv_ref[rows, D_KVH:2 * D_KVH] = va.astype(bf16)
        kv_ref[rows, 3 * D_KVH:] = vb.astype(bf16)
        for j in range(N_KV_A):
            cols = slice(j * HEAD_DIM, (j + 1) * HEAD_DIM)
            ka = head(qkv, rows, OFF_KA + j * HEAD_DIM, None, rope)
            kb = head(qkv, rows, OFF_KB + j * HEAD_DIM, kg, rope)
            kv_ref[rows, cols] = ka.astype(bf16)
            kv_ref[rows, 2 * D_KVH + j * HEAD_DIM:2 * D_KVH + (j + 1) * HEAD_DIM] = (
                kb.astype(bf16))
            if not rope:
                dst = pl.ds(j, SEQ, stride=N_KV_A)
                ca_ref[s, 0, dst, :] = ka
                ca_ref[s, 1, dst, :] = va[:, cols]
                cb_ref[s, 0, dst, :] = kb
                cb_ref[s, 1, dst, :] = vb[:, cols]

    def emit(rope):
        hs = []
        for s in range(TM // SEQ):
            x = x_refs[-1 if rope else 0][s * SEQ:(s + 1) * SEQ, :]
            hs.append((_ln(x) * (1.0 + scale_ref[...]) + shift_ref[...]).astype(bf16))
        qkvs = [jnp.dot(h, w_ref[...], preferred_element_type=f32) for h in hs]
        for s, qkv in enumerate(qkvs):
            emit_rows(qkv, s, rope)

    @pl.when(is_ctx)
    def _():
        emit(False)
        if n_alias == 0:
            rest = ca_ref.shape[1] - 2
            for c_ref in (ca_ref, cb_ref):
                c_ref[:, 2:] = jnp.zeros((TM // SEQ, rest, SEQ * N_KV_A, HEAD_DIM), f32)

    @pl.when(jnp.logical_not(is_ctx))
    def _():
        emit(True)


def _qkv(xs, mod, w_in, qg, kg, rope, caches, layer):
    rope_spec = pl.BlockSpec(
        (TM, HEAD_DIM), lambda i: (jnp.maximum(i - N_CTX_TILES, 0) % TILES_PER_LAT_SEQ, 0))
    assert caches or layer == 0
    slots, slot0 = (2 * DEPTH, 0) if not caches else (2, layer)
    cache_spec = pl.BlockSpec((TM // SEQ, slots, SEQ * N_KV_A, HEAD_DIM),
                              lambda i: (jnp.minimum(i, N_CTX_TILES - 1), slot0, 0, 0))
    cache_shape = jax.ShapeDtypeStruct((BATCH, DEPTH * 2, SEQ * N_KV_A, HEAD_DIM), f32)
    n_in = len(xs) + 8
    return pl.pallas_call(
        functools.partial(_qkv_kernel, n_x=len(xs), n_alias=len(caches)),
        grid=(N_TILES,),
        in_specs=_row_specs(len(xs) == 2) + [
            _mod_spec(layer, 0), _mod_spec(layer, 1),
            pl.BlockSpec((D_MODEL, D_IN), lambda i: (0, 0), pipeline_mode=pl.Buffered(1)),
            _layer_vec_spec(layer, HEAD_DIM), _layer_vec_spec(layer, HEAD_DIM),
            rope_spec, rope_spec, rope_spec,
        ] + [pl.BlockSpec(memory_space=pl.ANY)] * len(caches),
        out_specs=[
            pl.BlockSpec((TM, D_Q), lambda i: (i, 0)),
            pl.BlockSpec((TM, D_KV), lambda i: (i, 0)),
            cache_spec, cache_spec,
        ],
        out_shape=[
            jax.ShapeDtypeStruct((M_ALL, D_Q), bf16),
            jax.ShapeDtypeStruct((M_ALL, D_KV), bf16),
            cache_shape, cache_shape,
        ],
        input_output_aliases={n_in + k: 2 + k for k in range(len(caches))},
        compiler_params=_params("arbitrary"),
    )(*xs, mod, mod, w_in, qg, kg, *rope, *caches)


def _stack_heads(q_ref, first_head, rows=slice(None)):
    return jnp.concatenate(
        [q_ref[rows, (first_head + j) * HEAD_DIM:(first_head + j + 1) * HEAD_DIM]
         for j in range(GROUP)], axis=0)


def _sink_column(sink_ref, first, rows):
    return jnp.concatenate(
        [jnp.full((rows, 1), sink_ref[first + j] * LOG2E, f32) for j in range(GROUP)], axis=0)


def _logits(qs, k):
    return lax.dot_general(qs, k, (((1,), (1,)), ((), ())), preferred_element_type=f32)


def _with_ones(v):
    return jnp.concatenate([v, jnp.ones_like(v)], axis=1)


def _attend_all(jobs):
    ss = []
    for qs, k, _, bias, _ in jobs:
        s = _logits(qs, k)
        ss.append(s if bias is None else s + bias)
    ms = []
    for s, (_, _, _, _, sink) in zip(ss, jobs):
        m = jnp.max(s, axis=-1, keepdims=True)
        ms.append(m if sink is None else jnp.maximum(m, sink))
    ps = [jnp.exp2(s - m).astype(bf16) for s, m in zip(ss, ms)]
    outs = []
    for p, m, (_, _, v1, _, sink) in zip(ps, ms, jobs):
        o = jnp.dot(p, v1, preferred_element_type=f32)
        l = o[:, HEAD_DIM:]
        if sink is not None:
            l = l + jnp.exp2(sink - m)
        outs.append(o[:, :HEAD_DIM] * (1.0 / l))
    return outs


def _store_heads(o_ref, o, first_head, rows):
    for j in range(GROUP):
        c0 = (first_head + j) * HEAD_DIM
        o_ref[:, c0:c0 + HEAD_DIM] = o[j * rows:(j + 1) * rows].astype(o_ref.dtype)


QB = 2 * BLOCK
N_QB = DEC_SEQ // QB
WIN = QB + 2 * BLOCK
S_B = PAST_LEN + DEC_SEQ
A_PAD = BLOCK
A_ROWS = PAST_LEN + A_PAD + DEC_SEQ + A_PAD


def _window_bias():
    a = np.arange(QB)[:, None]
    j = np.arange(WIN)[None, :]
    near = (j >= a) & (j <= a + 2 * BLOCK)
    out = []
    for lo, hi in ((BLOCK, WIN), (0, WIN), (0, WIN - BLOCK)):
        ok = near & (j >= lo) & (j < hi)
        out.append(np.concatenate(
            [np.zeros((QB, PAST_LEN), np.float32), np.where(ok, 0.0, NEG).astype(np.float32)],
            axis=1))
    return jnp.asarray(np.stack(out))


def _lat_attn_kernel(sink_ref, q_ref, kva_ref, kvb_ref, ca_ref, cb_ref, bias_ref,
                     o_ref, ka_s, va_s, kb_s, vb_s, *, layer):
    n = pl.program_id(1)

    @pl.when(n == 0)
    def _():
        lat_a = slice(PAST_LEN + A_PAD, PAST_LEN + A_PAD + DEC_SEQ)
        ka_s[0:PAST_LEN, :] = ca_ref[0].astype(bf16)
        ka_s[lat_a, :] = kva_ref[:, 0:D_KVH]
        kb_s[0:PAST_LEN, :] = cb_ref[0].astype(bf16)
        kb_s[PAST_LEN:, :] = kvb_ref[:, 0:D_KVH]
        for pad0 in (PAST_LEN, PAST_LEN + A_PAD + DEC_SEQ):
            ka_s[pad0:pad0 + A_PAD, :] = jnp.zeros((A_PAD, D_KVH), bf16)
            va_s[pad0:pad0 + A_PAD, :] = jnp.zeros((A_PAD, 2 * D_KVH), bf16)
        for g in range(N_KV_A):
            src = slice(g * HEAD_DIM, (g + 1) * HEAD_DIM)
            vsrc = slice(D_KVH + g * HEAD_DIM, D_KVH + (g + 1) * HEAD_DIM)
            dst = slice(2 * g * HEAD_DIM, (2 * g + 1) * HEAD_DIM)
            one = slice((2 * g + 1) * HEAD_DIM, (2 * g + 2) * HEAD_DIM)
            va_s[0:PAST_LEN, dst] = ca_ref[1, :, src].astype(bf16)
            va_s[lat_a, dst] = kva_ref[:, vsrc]
            va_s[0:PAST_LEN, one] = jnp.ones((PAST_LEN, HEAD_DIM), bf16)
            va_s[lat_a, one] = jnp.ones((DEC_SEQ, HEAD_DIM), bf16)
            vb_s[0:PAST_LEN, dst] = cb_ref[1, :, src].astype(bf16)
            vb_s[PAST_LEN:, dst] = kvb_ref[:, vsrc]
            vb_s[:, one] = jnp.ones((S_B, HEAD_DIM), bf16)

    win = pl.ds(pl.multiple_of(PAST_LEN + n * QB, BLOCK), WIN)
    bias = jnp.concatenate([bias_ref[...]] * GROUP, axis=0)
    jobs, firsts = [], []
    for g in range(N_KV_A):
        kc = slice(g * HEAD_DIM, (g + 1) * HEAD_DIM)
        vc = slice(2 * g * HEAD_DIM, (2 * g + 2) * HEAD_DIM)
        k = jnp.concatenate([ka_s[0:PAST_LEN, kc], ka_s[win, kc]], axis=0)
        v1 = jnp.concatenate([va_s[0:PAST_LEN, vc], va_s[win, vc]], axis=0)
        sink = _sink_column(sink_ref, layer * N_HEADS_A + g * GROUP, QB)
        jobs.append((_stack_heads(q_ref, g * GROUP), k, v1, bias, sink))
        firsts.append(g * GROUP)

    for g in range(N_KV_B):
        kc = slice(g * HEAD_DIM, (g + 1) * HEAD_DIM)
        first = N_HEADS_A + g * GROUP
        jobs.append((_stack_heads(q_ref, first), kb_s[:, kc],
                     vb_s[:, 2 * g * HEAD_DIM:(2 * g + 2) * HEAD_DIM], None, None))
        firsts.append(first)
    for first, o in zip(firsts, _attend_all(jobs)):
        _store_heads(o_ref, o, first, QB)


def _lat_attn(sink, q, kv, cache_a, cache_b, bias, layer):
    blk0 = M_CTX // QB
    cache_spec = pl.BlockSpec((None, None, 2, PAST_LEN, D_KVH), lambda b, n: (b, layer, 0, 0, 0))
    half = 2 * D_KVH
    seq0 = M_CTX // DEC_SEQ
    return pl.pallas_call(
        functools.partial(_lat_attn_kernel, layer=layer),
        grid=(DEC_BATCH, N_QB),
        in_specs=[
            pl.BlockSpec(memory_space=pltpu.SMEM),
            pl.BlockSpec((QB, D_Q), lambda b, n: (blk0 + b * N_QB + n, 0)),
            pl.BlockSpec((DEC_SEQ, half), lambda b, n: (seq0 + b, 0)),
            pl.BlockSpec((DEC_SEQ, half), lambda b, n: (seq0 + b, 1)),
            cache_spec, cache_spec,
            pl.BlockSpec((None, QB, PAST_LEN + WIN),
                         lambda b, n: (jnp.where(n == 0, 0, jnp.where(n == N_QB - 1, 2, 1)), 0, 0)),
        ],
        out_specs=pl.BlockSpec((QB, D_Q), lambda b, n: (b * N_QB + n, 0)),
        out_shape=jax.ShapeDtypeStruct((M_LAT, D_Q), bf16),
        scratch_shapes=[
            pltpu.VMEM((A_ROWS, D_KVH), bf16), pltpu.VMEM((A_ROWS, 2 * D_KVH), bf16),
            pltpu.VMEM((S_B, D_KVH), bf16), pltpu.VMEM((S_B, 2 * D_KVH), bf16)],
        compiler_params=_params("arbitrary", "arbitrary"),
    )(sink, q, kv, kv, cache_a, cache_b, bias)


def _oproj_kernel(sink_ref, q_ref, kv_ref, ol_ref, *refs, n_x, layer):
    x_refs = refs[:n_x]
    w_ref, gate_ref, g_ref, b_ref, shift_ref, scale_ref, y_ref, h_ref = refs[n_x:]
    is_ctx = pl.program_id(0) < N_CTX_TILES
    subs = [slice(r0, r0 + SEQ) for r0 in range(0, TM, SEQ)]

    def ctx_attention(rows):
        heads = [None] * (N_HEADS_A + N_HEADS_B)
        for mixer in range(2):
            for g in range(2):
                kcol = mixer * 2 * D_KVH + g * HEAD_DIM
                k = kv_ref[rows, kcol:kcol + HEAD_DIM]
                v = kv_ref[rows, kcol + D_KVH:kcol + D_KVH + HEAD_DIM]
                first = mixer * N_HEADS_A + g * GROUP
                sink = None
                if mixer == 0:
                    sink = _sink_column(sink_ref, layer * N_HEADS_A + g * GROUP, SEQ)
                o, = _attend_all(
                    [(_stack_heads(q_ref, first, rows), k, _with_ones(v), None, sink)])
                for j in range(GROUP):
                    heads[first + j] = o[j * SEQ:(j + 1) * SEQ].astype(bf16)
        return jnp.concatenate(heads, axis=1)

    def finish(os, x_ref):
        fs = [jnp.dot(o, w_ref[...], preferred_element_type=f32) for o in os]
        for rows, f in zip(subs, fs):
            y = _ln(ALPHA * x_ref[rows, :] + gate_ref[...] * f) * g_ref[...] + b_ref[...]
            y_ref[rows, :] = y
            h_ref[rows, :] = (_ln(y) * (1.0 + scale_ref[...]) + shift_ref[...]).astype(bf16)

    @pl.when(is_ctx)
    def _():
        finish([ctx_attention(rows) for rows in subs], x_refs[0])

    @pl.when(jnp.logical_not(is_ctx))
    def _():
        finish([ol_ref[rows, :] for rows in subs], x_refs[-1])


def _oproj(sink, q, kv, o_lat, xs, w_o, mod, ln_g, ln_b, layer):
    ctx_tile = lambda i: (jnp.minimum(i, N_CTX_TILES - 1), 0)
    return pl.pallas_call(
        functools.partial(_oproj_kernel, n_x=len(xs), layer=layer),
        grid=(N_TILES,),
        in_specs=[
            pl.BlockSpec(memory_space=pltpu.SMEM),
            pl.BlockSpec((TM, D_Q), ctx_tile),
            pl.BlockSpec((TM, D_KV), ctx_tile),
            pl.BlockSpec((TM, D_Q), lambda i: (jnp.maximum(i - N_CTX_TILES, 0), 0)),
        ] + _row_specs(len(xs) == 2) + [
            pl.BlockSpec((D_Q, D_MODEL), lambda i: (0, 0), pipeline_mode=pl.Buffered(1)),
            _mod_spec(layer, 2), _layer_vec_spec(layer, D_MODEL), _layer_vec_spec(layer, D_MODEL),
            _mod_spec(layer, 3), _mod_spec(layer, 4),
        ],
        out_specs=[
            pl.BlockSpec((TM, D_MODEL), lambda i: (i, 0)),
            pl.BlockSpec((TM, D_MODEL), lambda i: (i, 0)),
        ],
        out_shape=[
            jax.ShapeDtypeStruct((M_ALL, D_MODEL), f32),
            jax.ShapeDtypeStruct((M_ALL, D_MODEL), bf16),
        ],
        compiler_params=_params("arbitrary"),
    )(sink, q, kv, o_lat, *xs, w_o, mod, ln_g, ln_b, mod, mod)


STEP_ROWS = 2 * TM
STEPS_PER_LAT_SEQ = DEC_SEQ // STEP_ROWS


def _conv_rows(u, cw, cb, is_ctx):
    mid_rows = slice(HALO, HALO + STEP_ROWS)
    um = pltpu.roll(u, 1, 0)[mid_rows]
    up = pltpu.roll(u, u.shape[0] - 1, 0)[mid_rows]
    lo, mid, hi = um * cw[0:1], u[mid_rows] * cw[1:2] + cb, up * cw[2:3]
    out = lo + mid + hi
    pieces, done = [], 0
    for b in range(SEQ, STEP_ROWS, SEQ):
        rows = slice(b - F32_ROWS, b + F32_ROWS)
        r = lax.broadcasted_iota(jnp.int32, (2 * F32_ROWS, u.shape[1]), 0) + (b - F32_ROWS)
        fixed = jnp.where(r == b - 1, lo[rows] + mid[rows],
                          jnp.where(r == b, mid[rows] + hi[rows], out[rows]))
        pieces += [out[done:b - F32_ROWS], jnp.where(is_ctx, fixed, out[rows])]
        done = b + F32_ROWS
    return jnp.concatenate(pieces + [out[done:]], axis=0)


def _ffn_up_kernel(hp_ref, h_ref, hn_ref, wv_ref, wg_ref, cwv_ref, cwg_ref, cbv_ref, cbg_ref,
                   wd_ref, a_ref, wdb_ref, w_s, hcat):
    j = pl.program_id(1)

    @pl.when(j == 0)
    def _():
        w_s[:, :TF] = wv_ref[...].astype(bf16)
        w_s[:, TF:] = wg_ref[...].astype(bf16)
        wdb_ref[...] = wd_ref[...].astype(bf16)

    n_ctx = M_CTX // STEP_ROWS
    is_ctx = j < n_ctx
    t = (j - n_ctx) % STEPS_PER_LAT_SEQ
    zero = jnp.zeros((HALO, D_MODEL), bf16)
    hcat[0:HALO, :] = jnp.where(is_ctx | (t == 0), zero, hp_ref[...])
    hcat[HALO:HALO + STEP_ROWS, :] = h_ref[...]
    hcat[HALO + STEP_ROWS:, :] = jnp.where(is_ctx | (t == STEPS_PER_LAT_SEQ - 1), zero, hn_ref[...])
    u = jnp.dot(hcat[...], w_s[...], preferred_element_type=f32)
    vs = _conv_rows(u[:, :TF], cwv_ref[...] * -LN2, cbv_ref[...] * -LN2, is_ctx)
    gs = _conv_rows(u[:, TF:], cwg_ref[...] * -LOG2E, cbg_ref[...] * -LOG2E, is_ctx)
    a_ref[...] = (gs * vs * (1.0 / (1.0 + jnp.exp2(gs)))).astype(bf16)


def _ffn_up(h, w_up, conv_w, conv_b, w_down, layer):
    hb = STEP_ROWS // HALO
    return pl.pallas_call(
        _ffn_up_kernel,
        grid=(N_F, M_ALL // STEP_ROWS),
        in_specs=[
            pl.BlockSpec((HALO, D_MODEL), lambda f, j: (jnp.maximum(j * hb - 1, 0), 0)),
            pl.BlockSpec((STEP_ROWS, D_MODEL), lambda f, j: (j, 0)),
            pl.BlockSpec((HALO, D_MODEL),
                         lambda f, j: (jnp.minimum((j + 1) * hb, M_ALL // HALO - 1), 0)),
            pl.BlockSpec((None, D_MODEL, TF), lambda f, j: (layer, 0, f)),
            pl.BlockSpec((None, D_MODEL, TF), lambda f, j: (layer, 0, N_F + f)),
            pl.BlockSpec((None, 3, TF), lambda f, j: (layer, 0, f)),
            pl.BlockSpec((None, 3, TF), lambda f, j: (layer, 0, N_F + f)),
            pl.BlockSpec((None, 1, TF), lambda f, j: (layer, 0, f)),
            pl.BlockSpec((None, 1, TF), lambda f, j: (layer, 0, N_F + f)),
            pl.BlockSpec((None, TF, D_MODEL), lambda f, j: (layer, f, 0)),
        ],
        out_specs=[pl.BlockSpec((STEP_ROWS, TF), lambda f, j: (j, f)),
                   pl.BlockSpec((TF, D_MODEL), lambda f, j: (f, 0))],
        out_shape=[jax.ShapeDtypeStruct((M_ALL, D_FF), bf16),
                   jax.ShapeDtypeStruct((D_FF, D_MODEL), bf16)],
        scratch_shapes=[pltpu.VMEM((D_MODEL, 2 * TF), bf16),
                        pltpu.VMEM((STEP_ROWS + 2 * HALO, D_MODEL), bf16)],
        compiler_params=_params("arbitrary", "arbitrary"),
    )(h, h, h, w_up, w_up, conv_w, conv_w, conv_b, conv_b, w_down)


def _ffn_down_kernel(a_ref, y_ref, w_ref, gate_ref, g_ref, b_ref, *o_refs):
    f = jnp.dot(a_ref[...], w_ref[...], preferred_element_type=f32)
    out = _ln(ALPHA * y_ref[...] + gate_ref[...] * f) * g_ref[...] + b_ref[...]
    if len(o_refs) == 1:
        o_refs[0][...] = out
    else:
        is_ctx = pl.program_id(0) < M_CTX // TM_DOWN

        @pl.when(is_ctx)
        def _():
            o_refs[0][...] = out

        @pl.when(jnp.logical_not(is_ctx))
        def _():
            o_refs[1][...] = out


def _ffn_down(a, y, w_down, mod, ln_g, ln_b, layer, split_out):
    if split_out:
        out_shape = [jax.ShapeDtypeStruct((M_CTX, D_MODEL), f32),
                     jax.ShapeDtypeStruct((M_LAT, D_MODEL), f32)]
    else:
        out_shape = [jax.ShapeDtypeStruct((M_ALL, D_MODEL), f32)]
    return pl.pallas_call(
        _ffn_down_kernel,
        grid=(M_ALL // TM_DOWN,),
        in_specs=[
            pl.BlockSpec((TM_DOWN, D_FF), lambda i: (i, 0)),
            pl.BlockSpec((TM_DOWN, D_MODEL), lambda i: (i, 0)),
            pl.BlockSpec((D_FF, D_MODEL), lambda i: (0, 0), pipeline_mode=pl.Buffered(1)),
            _mod_spec(layer, 5, tm=TM_DOWN),
            _layer_vec_spec(layer, D_MODEL), _layer_vec_spec(layer, D_MODEL),
        ],
        out_specs=_row_specs(split_out, tm=TM_DOWN),
        out_shape=out_shape,
        compiler_params=_params("arbitrary"),
    )(a, y, w_down, mod, ln_g, ln_b)


def _rope_tables():
    pos = np.arange(DEC_SEQ)
    q4 = HEAD_DIM // 4
    freq = np.float32(ROPE_THETA) ** (-np.arange(q4, dtype=np.float32) / np.float32(q4))
    row = ((pos // GRID_W).astype(np.float32)[:, None] * freq).astype(np.float64)
    col = ((pos % GRID_W).astype(np.float32)[:, None] * freq).astype(np.float64)
    zero = np.zeros_like(row)
    c = np.concatenate([np.cos(row), np.cos(row), np.cos(col), np.cos(col)], axis=1)
    a = np.concatenate([-np.sin(row), zero, -np.sin(col), zero], axis=1)
    b = np.concatenate([zero, np.sin(row), zero, np.sin(col)], axis=1)
    return tuple(jnp.asarray(t, f32) for t in (c, a, b))


def kernel(x_prompt, x_sample, cache_attn_a, cache_attn_b, c, c_ctx, w_ada, b_ada, w_in,
           q_norm_g, k_norm_g, sink_a, w_o, ln1_g, ln1_b, w_up, conv_w, conv_b, w_down,
           ln2_g, ln2_b):
    cvecs = jnp.concatenate(
        [c_ctx[None], c, jnp.zeros((N_MOD_ROWS - 1 - DEC_BATCH, D_MODEL), f32)], axis=0)
    mod = _adaln(cvecs, w_ada, b_ada).reshape(DEPTH * N_MOD_ROWS * 6, 1, D_MODEL)

    rope = _rope_tables()
    win_bias = _window_bias()
    sink = sink_a.reshape(DEPTH * N_HEADS_A)
    cache_a = cache_attn_a.reshape(DEC_BATCH, DEPTH, 2, PAST_LEN, D_KVH)
    cache_b = cache_attn_b.reshape(DEC_BATCH, DEPTH, 2, PAST_LEN, D_KVH)
    per_layer = lambda v: v.reshape(DEPTH, 1, v.shape[-1])
    qg, kg = per_layer(q_norm_g), per_layer(k_norm_g)
    g1, b1, g2, b2 = per_layer(ln1_g), per_layer(ln1_b), per_layer(ln2_g), per_layer(ln2_b)
    conv_b3 = per_layer(conv_b)

    xs = [x_prompt.reshape(M_CTX, D_MODEL), x_sample.reshape(M_LAT, D_MODEL)]
    new_caches = []
    for l in range(DEPTH):
        q, kv, new_a, new_b = _qkv(
            xs, mod, w_in[l].astype(bf16), qg, kg, rope, new_caches, l)
        new_caches = [new_a, new_b]
        o_lat = _lat_attn(sink, q, kv, cache_a, cache_b, win_bias, l)
        y, h = _oproj(sink, q, kv, o_lat, xs, w_o[l].astype(bf16), mod, g1, b1, l)
        act, w_down_b = _ffn_up(h, w_up, conv_w, conv_b3, w_down, l)
        xs = _ffn_down(act, y, w_down_b, mod, g2, b2, l, split_out=(l == DEPTH - 1))

    cache_shape = (BATCH, DEPTH, 2, SEQ, N_KV_A, HEAD_DIM)
    return (xs[0].reshape(BATCH, SEQ, D_MODEL), xs[1].reshape(DEC_BATCH, DEC_SEQ, D_MODEL),
            new_caches[0].reshape(cache_shape), new_caches[1].reshape(cache_shape))
```

```python
import functools

import jax
import jax.numpy as jnp
import numpy as np
from jax import lax
from jax.experimental import pallas as pl
from jax.experimental.pallas import tpu as pltpu

D_MODEL = 2048
BATCH = 32
SEQ = 256
DEPTH = 2
DEC_BATCH = 2
DEC_SEQ = 2048
PAST_LEN = 256
GRID_W = 64
HEAD_DIM = 128
N_HEADS_A = 8
N_KV_A = 2
N_HEADS_B = 8
N_KV_B = 2
GROUP = 4
BLOCK = 128
D_FF = 5632
ROPE_THETA = 10000.0
LN_EPS = 1e-6
ALPHA = (2.0 * DEPTH) ** 0.25
SCALE = HEAD_DIM ** -0.5
LOG2E = 1.4426950408889634
LN2 = 0.6931471805599453
QSCALE = SCALE * LOG2E
NEG = -1e30

D_Q = (N_HEADS_A + N_HEADS_B) * HEAD_DIM
D_KVH = N_KV_A * HEAD_DIM
D_KV = 4 * D_KVH
D_IN = D_Q + D_KV
OFF_QA = 0
OFF_KA = OFF_QA + N_HEADS_A * HEAD_DIM
OFF_VA = OFF_KA + D_KVH
OFF_QB = OFF_VA + D_KVH
OFF_KB = OFF_QB + N_HEADS_B * HEAD_DIM
OFF_VB = OFF_KB + D_KVH

M_CTX = BATCH * SEQ
M_LAT = DEC_BATCH * DEC_SEQ
M_ALL = M_CTX + M_LAT
N_MOD_ROWS = 8

TM = 2 * SEQ
N_TILES = M_ALL // TM
N_CTX_TILES = M_CTX // TM
TILES_PER_LAT_SEQ = DEC_SEQ // TM
TM_DOWN = 256
HALO = 16
TF = 512
N_F = D_FF // TF
TN_ADA = 2048
F32_ROWS = 8
VMEM_LIMIT = 60 * 1024 * 1024

f32 = jnp.float32
bf16 = jnp.bfloat16


def _params(*sem):
    return pltpu.CompilerParams(dimension_semantics=sem, vmem_limit_bytes=VMEM_LIMIT)


def _ln(x):
    mu = jnp.mean(x, axis=-1, keepdims=True)
    xc = x - mu
    var = jnp.mean(xc * xc, axis=-1, keepdims=True)
    return xc * lax.rsqrt(var + LN_EPS)


def _mod_spec(layer, which, tm=TM):
    n_ctx = M_CTX // tm
    per_seq = DEC_SEQ // tm

    def index_map(i):
        row = jnp.where(i < n_ctx, 0, 1 + (i - n_ctx) // per_seq)
        return ((layer * N_MOD_ROWS + row) * 6 + which, 0, 0)
    return pl.BlockSpec((None, 1, D_MODEL), index_map)


def _layer_vec_spec(layer, width):
    return pl.BlockSpec((None, 1, width), lambda i: (layer, 0, 0))


def _row_specs(split, tm=TM):
    n_ctx = M_CTX // tm
    if not split:
        return [pl.BlockSpec((tm, D_MODEL), lambda i: (i, 0))]
    return [pl.BlockSpec((tm, D_MODEL), lambda i: (jnp.minimum(i, n_ctx - 1), 0)),
            pl.BlockSpec((tm, D_MODEL), lambda i: (jnp.maximum(i - n_ctx, 0), 0))]


def _adaln_kernel(cv_ref, w_ref, b_ref, o_ref):
    cv = cv_ref[...]
    a = (cv * jax.nn.sigmoid(cv)).astype(bf16)
    o_ref[...] = jnp.dot(a, w_ref[...].astype(bf16), preferred_element_type=f32) + b_ref[...]


def _adaln(cvecs, w_ada, b_ada):
    return pl.pallas_call(
        _adaln_kernel,
        grid=(DEPTH, 6 * D_MODEL // TN_ADA),
        in_specs=[
            pl.BlockSpec((N_MOD_ROWS, D_MODEL), lambda l, n: (0, 0)),
            pl.BlockSpec((None, D_MODEL, TN_ADA), lambda l, n: (l, 0, n)),
            pl.BlockSpec((None, 1, TN_ADA), lambda l, n: (l, 0, n)),
        ],
        out_specs=pl.BlockSpec((None, N_MOD_ROWS, TN_ADA), lambda l, n: (l, 0, n)),
        out_shape=jax.ShapeDtypeStruct((DEPTH, N_MOD_ROWS, 6 * D_MODEL), f32),
        compiler_params=_params("arbitrary", "arbitrary"),
    )(cvecs, w_ada, b_ada.reshape(DEPTH, 1, 6 * D_MODEL))


def _qkv_kernel(*refs, n_x, n_alias):
    x_refs = refs[:n_x]
    (shift_ref, scale_ref, w_ref, qg_ref, kg_ref, rc_ref, ra_ref, rb_ref) = refs[n_x:n_x + 8]
    q_ref, kv_ref, ca_ref, cb_ref = refs[n_x + 8 + n_alias:]
    i = pl.program_id(0)
    is_ctx = i < N_CTX_TILES

    def head(qkv, rows, col, gain, rope):
        xh = qkv[:, col:col + HEAD_DIM]
        if gain is not None:
            ms = jnp.mean(xh * xh, axis=-1, keepdims=True)
            xh = xh * lax.rsqrt(ms + LN_EPS) * gain
        if rope:
            xh = (xh * rc_ref[rows, :] + pltpu.roll(xh, HEAD_DIM - 32, 1) * ra_ref[rows, :]
                  + pltpu.roll(xh, 32, 1) * rb_ref[rows, :])
        return xh

    def emit_rows(qkv, s, rope):
        rows = slice(s * SEQ, (s + 1) * SEQ)
        qg = qg_ref[...]
        kg = kg_ref[...]
        for hh in range(N_HEADS_A):
            q_ref[rows, hh * HEAD_DIM:(hh + 1) * HEAD_DIM] = (
                head(qkv, rows, OFF_QA + hh * HEAD_DIM, None, rope) * QSCALE).astype(bf16)
        for hh in range(N_HEADS_B):
            c0 = (N_HEADS_A + hh) * HEAD_DIM
            q_ref[rows, c0:c0 + HEAD_DIM] = (
                head(qkv, rows, OFF_QB + hh * HEAD_DIM, qg, rope) * QSCALE).astype(bf16)
        va = qkv[:, OFF_VA:OFF_VA + D_KVH]
        vb = qkv[:, OFF_VB:OFF_VB + D_KVH]
        kv_ref[rows, D_KVH:2 * D_KVH] = va.astype(bf16)
        kv_ref[rows, 3 * D_KVH:] = vb.astype(bf16)
        for j in range(N_KV_A):
            cols = slice(j * HEAD_DIM, (j + 1) * HEAD_DIM)
            ka = head(qkv, rows, OFF_KA + j * HEAD_DIM, None, rope)
            kb = head(qkv, rows, OFF_KB + j * HEAD_DIM, kg, rope)
            kv_ref[rows, cols] = ka.astype(bf16)
            kv_ref[rows, 2 * D_KVH + j * HEAD_DIM:2 * D_KVH + (j + 1) * HEAD_DIM] = (
                kb.astype(bf16))
            if not rope:
                dst = pl.ds(j, SEQ, stride=N_KV_A)
                ca_ref[s, 0, dst, :] = ka
                ca_ref[s, 1, dst, :] = va[:, cols]
                cb_ref[s, 0, dst, :] = kb
                cb_ref[s, 1, dst, :] = vb[:, cols]

    def emit(rope):
        hs = []
        for s in range(TM // SEQ):
            x = x_refs[-1 if rope else 0][s * SEQ:(s + 1) * SEQ, :]
            hs.append((_ln(x) * (1.0 + scale_ref[...]) + shift_ref[...]).astype(bf16))
        qkvs = [jnp.dot(h, w_ref[...], preferred_element_type=f32) for h in hs]
        for s, qkv in enumerate(qkvs):
            emit_rows(qkv, s, rope)

    @pl.when(is_ctx)
    def _():
        emit(False)
        if n_alias == 0:
            rest = ca_ref.shape[1] - 2
            for c_ref in (ca_ref, cb_ref):
                c_ref[:, 2:] = jnp.zeros((TM // SEQ, rest, SEQ * N_KV_A, HEAD_DIM), f32)

    @pl.when(jnp.logical_not(is_ctx))
    def _():
        emit(True)


def _qkv(xs, mod, w_in, qg, kg, rope, caches, layer):
    rope_spec = pl.BlockSpec(
        (TM, HEAD_DIM), lambda i: (jnp.maximum(i - N_CTX_TILES, 0) % TILES_PER_LAT_SEQ, 0))
    assert caches or layer == 0
    slots, slot0 = (2 * DEPTH, 0) if not caches else (2, layer)
    cache_spec = pl.BlockSpec((TM // SEQ, slots, SEQ * N_KV_A, HEAD_DIM),
                              lambda i: (jnp.minimum(i, N_CTX_TILES - 1), slot0, 0, 0))
    cache_shape = jax.ShapeDtypeStruct((BATCH, DEPTH * 2, SEQ * N_KV_A, HEAD_DIM), f32)
    n_in = len(xs) + 8
    return pl.pallas_call(
        functools.partial(_qkv_kernel, n_x=len(xs), n_alias=len(caches)),
        grid=(N_TILES,),
        in_specs=_row_specs(len(xs) == 2) + [
            _mod_spec(layer, 0), _mod_spec(layer, 1),
            pl.BlockSpec((None, D_MODEL, D_IN), lambda i: (layer, 0, 0),
                         pipeline_mode=pl.Buffered(1)),
            _layer_vec_spec(layer, HEAD_DIM), _layer_vec_spec(layer, HEAD_DIM),
            rope_spec, rope_spec, rope_spec,
        ] + [pl.BlockSpec(memory_space=pl.ANY)] * len(caches),
        out_specs=[
            pl.BlockSpec((TM, D_Q), lambda i: (i, 0)),
            pl.BlockSpec((TM, D_KV), lambda i: (i, 0)),
            cache_spec, cache_spec,
        ],
        out_shape=[
            jax.ShapeDtypeStruct((M_ALL, D_Q), bf16),
            jax.ShapeDtypeStruct((M_ALL, D_KV), bf16),
            cache_shape, cache_shape,
        ],
        input_output_aliases={n_in + k: 2 + k for k in range(len(caches))},
        compiler_params=_params("arbitrary"),
    )(*xs, mod, mod, w_in, qg, kg, *rope, *caches)


def _stack_heads(q_ref, first_head, rows=slice(None)):
    return jnp.concatenate(
        [q_ref[rows, (first_head + j) * HEAD_DIM:(first_head + j + 1) * HEAD_DIM]
         for j in range(GROUP)], axis=0)


def _sink_column(sink_ref, first, rows):
    return jnp.concatenate(
        [jnp.full((rows, 1), sink_ref[first + j] * LOG2E, f32) for j in range(GROUP)], axis=0)


def _logits(qs, k):
    return lax.dot_general(qs, k, (((1,), (1,)), ((), ())), preferred_element_type=f32)


def _with_ones(v):
    return jnp.concatenate([v, jnp.ones_like(v)], axis=1)


def _attend_all(jobs):
    ss = []
    for qs, k, _, bias, _ in jobs:
        s = _logits(qs, k)
        ss.append(s if bias is None else s + bias)
    ms = []
    for s, (_, _, _, _, sink) in zip(ss, jobs):
        m = jnp.max(s, axis=-1, keepdims=True)
        ms.append(m if sink is None else jnp.maximum(m, sink))
    ps = [jnp.exp2(s - m).astype(bf16) for s, m in zip(ss, ms)]
    outs = []
    for p, m, (_, _, v1, _, sink) in zip(ps, ms, jobs):
        o = jnp.dot(p, v1, preferred_element_type=f32)
        l = o[:, HEAD_DIM:]
        if sink is not None:
            l = l + jnp.exp2(sink - m)
        outs.append(o[:, :HEAD_DIM] * (1.0 / l))
    return outs


def _store_heads(o_ref, o, first_head, rows):
    for j in range(GROUP):
        c0 = (first_head + j) * HEAD_DIM
        o_ref[:, c0:c0 + HEAD_DIM] = o[j * rows:(j + 1) * rows].astype(o_ref.dtype)


QB = 2 * BLOCK
N_QB = DEC_SEQ // QB
WIN = QB + 2 * BLOCK
S_B = PAST_LEN + DEC_SEQ
A_PAD = BLOCK
A_ROWS = PAST_LEN + A_PAD + DEC_SEQ + A_PAD


def _window_bias():
    a = np.arange(QB)[:, None]
    j = np.arange(WIN)[None, :]
    near = (j >= a) & (j <= a + 2 * BLOCK)
    out = []
    for lo, hi in ((BLOCK, WIN), (0, WIN), (0, WIN - BLOCK)):
        ok = near & (j >= lo) & (j < hi)
        out.append(np.concatenate(
            [np.zeros((QB, PAST_LEN), np.float32), np.where(ok, 0.0, NEG).astype(np.float32)],
            axis=1))
    return jnp.asarray(np.stack(out))


def _lat_attn_kernel(sink_ref, q_ref, kva_ref, kvb_ref, ca_ref, cb_ref, bias_ref,
                     o_ref, ka_s, va_s, kb_s, vb_s, *, layer):
    n = pl.program_id(1)

    @pl.when(n == 0)
    def _():
        lat_a = slice(PAST_LEN + A_PAD, PAST_LEN + A_PAD + DEC_SEQ)
        ka_s[0:PAST_LEN, :] = ca_ref[0].astype(bf16)
        ka_s[lat_a, :] = kva_ref[:, 0:D_KVH]
        kb_s[0:PAST_LEN, :] = cb_ref[0].astype(bf16)
        kb_s[PAST_LEN:, :] = kvb_ref[:, 0:D_KVH]
        for pad0 in (PAST_LEN, PAST_LEN + A_PAD + DEC_SEQ):
            ka_s[pad0:pad0 + A_PAD, :] = jnp.zeros((A_PAD, D_KVH), bf16)
            va_s[pad0:pad0 + A_PAD, :] = jnp.zeros((A_PAD, 2 * D_KVH), bf16)
        for g in range(N_KV_A):
            src = slice(g * HEAD_DIM, (g + 1) * HEAD_DIM)
            vsrc = slice(D_KVH + g * HEAD_DIM, D_KVH + (g + 1) * HEAD_DIM)
            dst = slice(2 * g * HEAD_DIM, (2 * g + 1) * HEAD_DIM)
            one = slice((2 * g + 1) * HEAD_DIM, (2 * g + 2) * HEAD_DIM)
            va_s[0:PAST_LEN, dst] = ca_ref[1, :, src].astype(bf16)
            va_s[lat_a, dst] = kva_ref[:, vsrc]
            va_s[0:PAST_LEN, one] = jnp.ones((PAST_LEN, HEAD_DIM), bf16)
            va_s[lat_a, one] = jnp.ones((DEC_SEQ, HEAD_DIM), bf16)
            vb_s[0:PAST_LEN, dst] = cb_ref[1, :, src].astype(bf16)
            vb_s[PAST_LEN:, dst] = kvb_ref[:, vsrc]
            vb_s[:, one] = jnp.ones((S_B, HEAD_DIM), bf16)

    win = pl.ds(pl.multiple_of(PAST_LEN + n * QB, BLOCK), WIN)
    bias = jnp.concatenate([bias_ref[...]] * GROUP, axis=0)
    jobs, firsts = [], []
    for g in range(N_KV_A):
        kc = slice(g * HEAD_DIM, (g + 1) * HEAD_DIM)
        vc = slice(2 * g * HEAD_DIM, (2 * g + 2) * HEAD_DIM)
        k = jnp.concatenate([ka_s[0:PAST_LEN, kc], ka_s[win, kc]], axis=0)
        v1 = jnp.concatenate([va_s[0:PAST_LEN, vc], va_s[win, vc]], axis=0)
        sink = _sink_column(sink_ref, layer * N_HEADS_A + g * GROUP, QB)
        jobs.append((_stack_heads(q_ref, g * GROUP), k, v1, bias, sink))
        firsts.append(g * GROUP)

    for g in range(N_KV_B):
        kc = slice(g * HEAD_DIM, (g + 1) * HEAD_DIM)
        first = N_HEADS_A + g * GROUP
        jobs.append((_stack_heads(q_ref, first), kb_s[:, kc],
                     vb_s[:, 2 * g * HEAD_DIM:(2 * g + 2) * HEAD_DIM], None, None))
        firsts.append(first)
    for first, o in zip(firsts, _attend_all(jobs)):
        _store_heads(o_ref, o, first, QB)


def _lat_attn(sink, q, kv, cache_a, cache_b, bias, layer):
    blk0 = M_CTX // QB
    cache_spec = pl.BlockSpec((None, None, 2, PAST_LEN, D_KVH), lambda b, n: (b, layer, 0, 0, 0))
    half = 2 * D_KVH
    seq0 = M_CTX // DEC_SEQ
    return pl.pallas_call(
        functools.partial(_lat_attn_kernel, layer=layer),
        grid=(DEC_BATCH, N_QB),
        in_specs=[
            pl.BlockSpec(memory_space=pltpu.SMEM),
            pl.BlockSpec((QB, D_Q), lambda b, n: (blk0 + b * N_QB + n, 0)),
            pl.BlockSpec((DEC_SEQ, half), lambda b, n: (seq0 + b, 0)),
            pl.BlockSpec((DEC_SEQ, half), lambda b, n: (seq0 + b, 1)),
            cache_spec, cache_spec,
            pl.BlockSpec((None, QB, PAST_LEN + WIN),
                         lambda b, n: (jnp.where(n == 0, 0, jnp.where(n == N_QB - 1, 2, 1)), 0, 0)),
        ],
        out_specs=pl.BlockSpec((QB, D_Q), lambda b, n: (b * N_QB + n, 0)),
        out_shape=jax.ShapeDtypeStruct((M_LAT, D_Q), bf16),
        scratch_shapes=[
            pltpu.VMEM((A_ROWS, D_KVH), bf16), pltpu.VMEM((A_ROWS, 2 * D_KVH), bf16),
            pltpu.VMEM((S_B, D_KVH), bf16), pltpu.VMEM((S_B, 2 * D_KVH), bf16)],
        compiler_params=_params("arbitrary", "arbitrary"),
    )(sink, q, kv, kv, cache_a, cache_b, bias)


def _oproj_kernel(sink_ref, q_ref, kv_ref, ol_ref, *refs, n_x, layer):
    x_refs = refs[:n_x]
    w_ref, gate_ref, g_ref, b_ref, shift_ref, scale_ref, y_ref, h_ref = refs[n_x:]
    is_ctx = pl.program_id(0) < N_CTX_TILES
    subs = [slice(r0, r0 + SEQ) for r0 in range(0, TM, SEQ)]

    def ctx_attention(rows):
        heads = [None] * (N_HEADS_A + N_HEADS_B)
        for mixer in range(2):
            for g in range(2):
                kcol = mixer * 2 * D_KVH + g * HEAD_DIM
                k = kv_ref[rows, kcol:kcol + HEAD_DIM]
                v = kv_ref[rows, kcol + D_KVH:kcol + D_KVH + HEAD_DIM]
                first = mixer * N_HEADS_A + g * GROUP
                sink = None
                if mixer == 0:
                    sink = _sink_column(sink_ref, layer * N_HEADS_A + g * GROUP, SEQ)
                o, = _attend_all(
                    [(_stack_heads(q_ref, first, rows), k, _with_ones(v), None, sink)])
                for j in range(GROUP):
                    heads[first + j] = o[j * SEQ:(j + 1) * SEQ].astype(bf16)
        return jnp.concatenate(heads, axis=1)

    def finish(os, x_ref):
        fs = [jnp.dot(o, w_ref[...], preferred_element_type=f32) for o in os]
        for rows, f in zip(subs, fs):
            y = _ln(ALPHA * x_ref[rows, :] + gate_ref[...] * f) * g_ref[...] + b_ref[...]
            y_ref[rows, :] = y
            h_ref[rows, :] = (_ln(y) * (1.0 + scale_ref[...]) + shift_ref[...]).astype(bf16)

    @pl.when(is_ctx)
    def _():
        finish([ctx_attention(rows) for rows in subs], x_refs[0])

    @pl.when(jnp.logical_not(is_ctx))
    def _():
        finish([ol_ref[rows, :] for rows in subs], x_refs[-1])


def _oproj(sink, q, kv, o_lat, xs, w_o, mod, ln_g, ln_b, layer):
    ctx_tile = lambda i: (jnp.minimum(i, N_CTX_TILES - 1), 0)
    return pl.pallas_call(
        functools.partial(_oproj_kernel, n_x=len(xs), layer=layer),
        grid=(N_TILES,),
        in_specs=[
            pl.BlockSpec(memory_space=pltpu.SMEM),
            pl.BlockSpec((TM, D_Q), ctx_tile),
            pl.BlockSpec((TM, D_KV), ctx_tile),
            pl.BlockSpec((TM, D_Q), lambda i: (jnp.maximum(i - N_CTX_TILES, 0), 0)),
        ] + _row_specs(len(xs) == 2) + [
            pl.BlockSpec((None, D_Q, D_MODEL), lambda i: (layer, 0, 0),
                         pipeline_mode=pl.Buffered(1)),
            _mod_spec(layer, 2), _layer_vec_spec(layer, D_MODEL), _layer_vec_spec(layer, D_MODEL),
            _mod_spec(layer, 3), _mod_spec(layer, 4),
        ],
        out_specs=[
            pl.BlockSpec((TM, D_MODEL), lambda i: (i, 0)),
            pl.BlockSpec((TM, D_MODEL), lambda i: (i, 0)),
        ],
        out_shape=[
            jax.ShapeDtypeStruct((M_ALL, D_MODEL), f32),
            jax.ShapeDtypeStruct((M_ALL, D_MODEL), bf16),
        ],
        compiler_params=_params("arbitrary"),
    )(sink, q, kv, o_lat, *xs, w_o, mod, ln_g, ln_b, mod, mod)


STEP_ROWS = 2 * TM
STEPS_PER_LAT_SEQ = DEC_SEQ // STEP_ROWS


def _conv_rows(u, cw, cb, is_ctx):
    mid_rows = slice(HALO, HALO + STEP_ROWS)
    um = pltpu.roll(u, 1, 0)[mid_rows]
    up = pltpu.roll(u, u.shape[0] - 1, 0)[mid_rows]
    lo, mid, hi = um * cw[0:1], u[mid_rows] * cw[1:2] + cb, up * cw[2:3]
    out = lo + mid + hi
    pieces, done = [], 0
    for b in range(SEQ, STEP_ROWS, SEQ):
        rows = slice(b - F32_ROWS, b + F32_ROWS)
        r = lax.broadcasted_iota(jnp.int32, (2 * F32_ROWS, u.shape[1]), 0) + (b - F32_ROWS)
        fixed = jnp.where(r == b - 1, lo[rows] + mid[rows],
                          jnp.where(r == b, mid[rows] + hi[rows], out[rows]))
        pieces += [out[done:b - F32_ROWS], jnp.where(is_ctx, fixed, out[rows])]
        done = b + F32_ROWS
    return jnp.concatenate(pieces + [out[done:]], axis=0)


def _ffn_up_kernel(hp_ref, h_ref, hn_ref, wv_ref, wg_ref, cwv_ref, cwg_ref, cbv_ref, cbg_ref,
                   wd_ref, a_ref, wdb_ref, w_s, hcat):
    j = pl.program_id(1)

    @pl.when(j == 0)
    def _():
        w_s[:, :TF] = wv_ref[...].astype(bf16)
        w_s[:, TF:] = wg_ref[...].astype(bf16)
        wdb_ref[...] = wd_ref[...].astype(bf16)

    n_ctx = M_CTX // STEP_ROWS
    is_ctx = j < n_ctx
    t = (j - n_ctx) % STEPS_PER_LAT_SEQ
    zero = jnp.zeros((HALO, D_MODEL), bf16)
    hcat[0:HALO, :] = jnp.where(is_ctx | (t == 0), zero, hp_ref[...])
    hcat[HALO:HALO + STEP_ROWS, :] = h_ref[...]
    hcat[HALO + STEP_ROWS:, :] = jnp.where(is_ctx | (t == STEPS_PER_LAT_SEQ - 1), zero, hn_ref[...])
    u = jnp.dot(hcat[...], w_s[...], preferred_element_type=f32)
    vs = _conv_rows(u[:, :TF], cwv_ref[...] * -LN2, cbv_ref[...] * -LN2, is_ctx)
    gs = _conv_rows(u[:, TF:], cwg_ref[...] * -LOG2E, cbg_ref[...] * -LOG2E, is_ctx)
    a_ref[...] = (gs * vs * (1.0 / (1.0 + jnp.exp2(gs)))).astype(bf16)


def _ffn_up(h, w_up, conv_w, conv_b, w_down, layer):
    hb = STEP_ROWS // HALO
    return pl.pallas_call(
        _ffn_up_kernel,
        grid=(N_F, M_ALL // STEP_ROWS),
        in_specs=[
            pl.BlockSpec((HALO, D_MODEL), lambda f, j: (jnp.maximum(j * hb - 1, 0), 0)),
            pl.BlockSpec((STEP_ROWS, D_MODEL), lambda f, j: (j, 0)),
            pl.BlockSpec((HALO, D_MODEL),
                         lambda f, j: (jnp.minimum((j + 1) * hb, M_ALL // HALO - 1), 0)),
            pl.BlockSpec((None, D_MODEL, TF), lambda f, j: (layer, 0, f)),
            pl.BlockSpec((None, D_MODEL, TF), lambda f, j: (layer, 0, N_F + f)),
            pl.BlockSpec((None, 3, TF), lambda f, j: (layer, 0, f)),
            pl.BlockSpec((None, 3, TF), lambda f, j: (layer, 0, N_F + f)),
            pl.BlockSpec((None, 1, TF), lambda f, j: (layer, 0, f)),
            pl.BlockSpec((None, 1, TF), lambda f, j: (layer, 0, N_F + f)),
            pl.BlockSpec((None, TF, D_MODEL), lambda f, j: (layer, f, 0)),
        ],
        out_specs=[pl.BlockSpec((STEP_ROWS, TF), lambda f, j: (j, f)),
                   pl.BlockSpec((TF, D_MODEL), lambda f, j: (f, 0))],
        out_shape=[jax.ShapeDtypeStruct((M_ALL, D_FF), bf16),
                   jax.ShapeDtypeStruct((D_FF, D_MODEL), bf16)],
        scratch_shapes=[pltpu.VMEM((D_MODEL, 2 * TF), bf16),
                        pltpu.VMEM((STEP_ROWS + 2 * HALO, D_MODEL), bf16)],
        compiler_params=_params("arbitrary", "arbitrary"),
    )(h, h, h, w_up, w_up, conv_w, conv_w, conv_b, conv_b, w_down)


def _ffn_down_kernel(a_ref, y_ref, w_ref, gate_ref, g_ref, b_ref, *o_refs):
    f = jnp.dot(a_ref[...], w_ref[...], preferred_element_type=f32)
    out = _ln(ALPHA * y_ref[...] + gate_ref[...] * f) * g_ref[...] + b_ref[...]
    if len(o_refs) == 1:
        o_refs[0][...] = out
    else:
        is_ctx = pl.program_id(0) < M_CTX // TM_DOWN

        @pl.when(is_ctx)
        def _():
            o_refs[0][...] = out

        @pl.when(jnp.logical_not(is_ctx))
        def _():
            o_refs[1][...] = out


def _ffn_down(a, y, w_down, mod, ln_g, ln_b, layer, split_out):
    if split_out:
        out_shape = [jax.ShapeDtypeStruct((M_CTX, D_MODEL), f32),
                     jax.ShapeDtypeStruct((M_LAT, D_MODEL), f32)]
    else:
        out_shape = [jax.ShapeDtypeStruct((M_ALL, D_MODEL), f32)]
    return pl.pallas_call(
        _ffn_down_kernel,
        grid=(M_ALL // TM_DOWN,),
        in_specs=[
            pl.BlockSpec((TM_DOWN, D_FF), lambda i: (i, 0)),
            pl.BlockSpec((TM_DOWN, D_MODEL), lambda i: (i, 0)),
            pl.BlockSpec((D_FF, D_MODEL), lambda i: (0, 0), pipeline_mode=pl.Buffered(1)),
            _mod_spec(layer, 5, tm=TM_DOWN),
            _layer_vec_spec(layer, D_MODEL), _layer_vec_spec(layer, D_MODEL),
        ],
        out_specs=_row_specs(split_out, tm=TM_DOWN),
        out_shape=out_shape,
        compiler_params=_params("arbitrary"),
    )(a, y, w_down, mod, ln_g, ln_b)


def _rope_tables():
    pos = np.arange(DEC_SEQ)
    q4 = HEAD_DIM // 4
    freq = np.float32(ROPE_THETA) ** (-np.arange(q4, dtype=np.float32) / np.float32(q4))
    row = ((pos // GRID_W).astype(np.float32)[:, None] * freq).astype(np.float64)
    col = ((pos % GRID_W).astype(np.float32)[:, None] * freq).astype(np.float64)
    zero = np.zeros_like(row)
    c = np.concatenate([np.cos(row), np.cos(row), np.cos(col), np.cos(col)], axis=1)
    a = np.concatenate([-np.sin(row), zero, -np.sin(col), zero], axis=1)
    b = np.concatenate([zero, np.sin(row), zero, np.sin(col)], axis=1)
    return tuple(jnp.asarray(t, f32) for t in (c, a, b))


def kernel(x_prompt, x_sample, cache_attn_a, cache_attn_b, c, c_ctx, w_ada, b_ada, w_in,
           q_norm_g, k_norm_g, sink_a, w_o, ln1_g, ln1_b, w_up, conv_w, conv_b, w_down,
           ln2_g, ln2_b):
    cvecs = jnp.concatenate(
        [c_ctx[None], c, jnp.zeros((N_MOD_ROWS - 1 - DEC_BATCH, D_MODEL), f32)], axis=0)
    mod = _adaln(cvecs, w_ada, b_ada).reshape(DEPTH * N_MOD_ROWS * 6, 1, D_MODEL)

    w_in_b = w_in.astype(bf16)
    w_o_b = w_o.astype(bf16)
    rope = _rope_tables()
    win_bias = _window_bias()
    sink = sink_a.reshape(DEPTH * N_HEADS_A)
    cache_a = cache_attn_a.reshape(DEC_BATCH, DEPTH, 2, PAST_LEN, D_KVH)
    cache_b = cache_attn_b.reshape(DEC_BATCH, DEPTH, 2, PAST_LEN, D_KVH)
    per_layer = lambda v: v.reshape(DEPTH, 1, v.shape[-1])
    qg, kg = per_layer(q_norm_g), per_layer(k_norm_g)
    g1, b1, g2, b2 = per_layer(ln1_g), per_layer(ln1_b), per_layer(ln2_g), per_layer(ln2_b)
    conv_b3 = per_layer(conv_b)

    xs = [x_prompt.reshape(M_CTX, D_MODEL), x_sample.reshape(M_LAT, D_MODEL)]
    new_caches = []
    for l in range(DEPTH):
        q, kv, new_a, new_b = _qkv(
            xs, mod, w_in_b, qg, kg, rope, new_caches, l)
        new_caches = [new_a, new_b]
        o_lat = _lat_attn(sink, q, kv, cache_a, cache_b, win_bias, l)
        y, h = _oproj(sink, q, kv, o_lat, xs, w_o_b, mod, g1, b1, l)
        act, w_down_b = _ffn_up(h, w_up, conv_w, conv_b3, w_down, l)
        xs = _ffn_down(act, y, w_down_b, mod, g2, b2, l, split_out=(l == DEPTH - 1))

    cache_shape = (BATCH, DEPTH, 2, SEQ, N_KV_A, HEAD_DIM)
    return (xs[0].reshape(BATCH, SEQ, D_MODEL), xs[1].reshape(DEC_BATCH, DEC_SEQ, D_MODEL),
            new_caches[0].reshape(cache_shape), new_caches[1].reshape(cache_shape))
```
